```python
import math
import jax, jax.numpy as jnp
from jax import lax
import numpy as np

D_MODEL = 2048
BATCH = 2
SEQ = 4096
DEPTH = 2

GRID_W = 64
ROPE_THETA = 500000.0
Q_BLOCK = 128

MLA_HEADS = 8
MLA_Q_RANK = 512
MLA_KV_RANK = 256
MLA_NOPE = 128
MLA_ROPE = 64
MLA_V = 128

NA_HEADS = 4
NA_DIM = 128
NA_WIN_H = 8
NA_WIN_W = 16

DIFF_HEADS = 4
DIFF_QK = 64
DIFF_V = 2 * DIFF_QK
DIFF_ROT = DIFF_QK // 4

MIX_A = MLA_HEADS * MLA_V
MIX_B = NA_HEADS * NA_DIM
MIX_C = DIFF_HEADS * DIFF_V
MIX_WIDTH = MIX_A + MIX_B + MIX_C

IN_A = MLA_Q_RANK + MLA_KV_RANK + MLA_ROPE
IN_B = 3 * NA_HEADS * NA_DIM
IN_C = 2 * DIFF_HEADS * 2 * DIFF_QK + DIFF_HEADS * DIFF_V
IN_WIDTH = IN_A + IN_B + IN_C

N_EXPERTS = 64
TOP_K = 8
N_GROUPS = 8
TOPK_GROUPS = 4
EXPERT_FF = 512
SHARED_FF = 512
ROUTED_SCALE = 2.5

DN_ALPHA = (2 * DEPTH) ** 0.25
DN_BETA = (8 * DEPTH) ** -0.25
LN_EPS = 1e-5
RMS_EPS = 1e-6

kernel_name = "hybrid_mla_natten_diff_moe_encoder"


def _layernorm(x, g=None, b=None):
    xf = x.astype(jnp.float32)
    mu = jnp.mean(xf, -1, keepdims=True)
    var = jnp.mean(jnp.square(xf - mu), -1, keepdims=True)
    y = (xf - mu) * lax.rsqrt(var + LN_EPS)
    if g is not None:
        y = y * g.astype(jnp.float32) + b.astype(jnp.float32)
    return y.astype(x.dtype)


def _rmsnorm(x, g):
    xf = x.astype(jnp.float32)
    y = xf * lax.rsqrt(jnp.mean(jnp.square(xf), -1, keepdims=True) + RMS_EPS)
    return (y * g.astype(jnp.float32)).astype(x.dtype)


def _rope(x, pos, rot_dim):
    half = rot_dim // 2
    inv = ROPE_THETA ** (-jnp.arange(half, dtype=jnp.float32) / half)
    ang = pos.astype(jnp.float32)[:, None] * inv[None, :]
    shape = (ang.shape[0],) + (1,) * (x.ndim - 3) + (half,)
    cos = jnp.cos(ang).reshape(shape)
    sin = jnp.sin(ang).reshape(shape)
    xr = x[..., :rot_dim].astype(jnp.float32)
    x1, x2 = xr[..., :half], xr[..., half:]
    rot = jnp.concatenate([x1 * cos - x2 * sin, x2 * cos + x1 * sin], -1).astype(x.dtype)
    return jnp.concatenate([rot, x[..., rot_dim:]], -1)


def _sweep_query_blocks(fn, *qs):
    b, s = qs[0].shape[:2]
    nblk = s // Q_BLOCK
    blk = tuple(jnp.moveaxis(q.reshape((b, nblk, Q_BLOCK) + q.shape[2:]), 1, 0) for q in qs)
    out = lax.map(lambda args: fn(*args), blk)
    return jnp.moveaxis(out, 0, 1).reshape((b, s) + out.shape[3:])


def _mla(u, pos, q_norm, w_uq, kv_norm, w_ukv):
    b, s, _ = u.shape
    cq, ckv, kr = jnp.split(u, [MLA_Q_RANK, MLA_Q_RANK + MLA_KV_RANK], -1)
    q = (_rmsnorm(cq, q_norm) @ w_uq).reshape(b, s, MLA_HEADS, MLA_NOPE + MLA_ROPE)
    q_nope = q[..., :MLA_NOPE]
    q_rope = _rope(q[..., MLA_NOPE:], pos, MLA_ROPE)
    kv = (_rmsnorm(ckv, kv_norm) @ w_ukv).reshape(b, s, MLA_HEADS, MLA_NOPE + MLA_V)
    k_nope, v = kv[..., :MLA_NOPE], kv[..., MLA_NOPE:]
    k_rope = _rope(kr, pos, MLA_ROPE)
    scale = (MLA_NOPE + MLA_ROPE) ** -0.5

    def block(qn, qr):
        sc = (jnp.einsum('bqhd,bkhd->bhqk', qn, k_nope)
              + jnp.einsum('bqhr,bkr->bhqk', qr, k_rope))
        p = jax.nn.softmax(sc.astype(jnp.float32) * scale, axis=-1).astype(v.dtype)
        return jnp.einsum('bhqk,bkhd->bqhd', p, v)

    o = _sweep_query_blocks(block, q_nope, q_rope)
    return o.reshape(b, s, MIX_A)


def _neighbourhood_attn(u, rpb):
    b, s, _ = u.shape
    rows = s // GRID_W
    kh = min(NA_WIN_H, rows)
    q, k, v = jnp.split(u, 3, -1)
    q, k, v = (t.reshape(b, rows, GRID_W, NA_HEADS, NA_DIM) for t in (q, k, v))
    r = jnp.arange(rows)
    r0 = jnp.clip(r - kh // 2, 0, rows - kh)
    key_rows = r0[:, None] + jnp.arange(kh)[None, :]
    k_blk = k[:, key_rows].reshape(b, rows, kh * GRID_W, NA_HEADS, NA_DIM)
    v_blk = v[:, key_rows].reshape(b, rows, kh * GRID_W, NA_HEADS, NA_DIM)
    col = jnp.arange(GRID_W)
    c0 = jnp.clip(col - NA_WIN_W // 2, 0, GRID_W - NA_WIN_W)
    col_in = (col[None, :] >= c0[:, None]) & (col[None, :] < c0[:, None] + NA_WIN_W)
    mask = jnp.broadcast_to(col_in[:, None, :], (GRID_W, kh, GRID_W)).reshape(GRID_W, kh * GRID_W)
    dr = key_rows - r[:, None] + (NA_WIN_H - 1)
    dc = jnp.clip(col[None, :] - col[:, None], -(NA_WIN_W - 1), NA_WIN_W - 1) + (NA_WIN_W - 1)
    bias = rpb[:, dr[:, None, :, None], dc[None, :, None, :]]
    bias = bias.reshape(NA_HEADS, rows, GRID_W, kh * GRID_W).astype(jnp.float32)
    sc = jnp.einsum('brqhd,brkhd->bhrqk', q, k_blk).astype(jnp.float32) * (NA_DIM ** -0.5) + bias
    sc = jnp.where(mask, sc, -jnp.inf)
    p = jax.nn.softmax(sc, axis=-1).astype(v.dtype)
    o = jnp.einsum('bhrqk,brkhd->brqhd', p, v_blk)
    return o.reshape(b, s, MIX_B)


def _diff_attn(u, pos, lq1, lk1, lq2, lk2, subln, lam_init):
    b, s, _ = u.shape
    nq = DIFF_HEADS * 2 * DIFF_QK
    q, k, v = jnp.split(u, [nq, 2 * nq], -1)
    q = _rope(q.reshape(b, s, DIFF_HEADS, 2, DIFF_QK), pos, DIFF_ROT)
    k = _rope(k.reshape(b, s, DIFF_HEADS, 2, DIFF_QK), pos, DIFF_ROT)
    v = v.reshape(b, s, DIFF_HEADS, DIFF_V)
    lam = (jnp.exp(jnp.sum(lq1.astype(jnp.float32) * lk1.astype(jnp.float32)))
           - jnp.exp(jnp.sum(lq2.astype(jnp.float32) * lk2.astype(jnp.float32))) + lam_init)
    k1, k2 = k[..., 0, :], k[..., 1, :]
    scale = DIFF_QK ** -0.5

    def block(q1, q2):
        a1 = jax.nn.softmax(jnp.einsum('bqhd,bkhd->bhqk', q1, k1).astype(jnp.float32) * scale, axis=-1)
        a2 = jax.nn.softmax(jnp.einsum('bqhd,bkhd->bhqk', q2, k2).astype(jnp.float32) * scale, axis=-1)
        return jnp.einsum('bhqk,bkhd->bqhd', (a1 - lam * a2).astype(v.dtype), v)

    o = _sweep_query_blocks(block, q[..., 0, :], q[..., 1, :])
    o = _rmsnorm(o, subln) * (1.0 - lam_init)
    return o.reshape(b, s, MIX_C)


def _moe(h, router_w, router_bias, w_gate, w_up, w_down, sh_gate, sh_up, sh_down):
    b, s, d = h.shape
    t = h.reshape(b * s, d)
    scores = jax.nn.sigmoid((t @ router_w).astype(jnp.float32))
    biased = scores + router_bias.astype(jnp.float32)
    grp = biased.reshape(-1, N_GROUPS, N_EXPERTS // N_GROUPS)
    grp_score = jnp.sum(lax.top_k(grp, 2)[0], -1)
    _, gidx = lax.top_k(grp_score, TOPK_GROUPS)
    gmask = jnp.sum(jax.nn.one_hot(gidx, N_GROUPS, dtype=jnp.float32), 1) > 0
    emask = jnp.repeat(gmask, N_EXPERTS // N_GROUPS, axis=1)
    _, eidx = lax.top_k(jnp.where(emask, biased, -jnp.inf), TOP_K)
    w = jnp.take_along_axis(scores, eidx, -1)
    w = w / jnp.sum(w, -1, keepdims=True) * ROUTED_SCALE
    gates = jnp.sum(jax.nn.one_hot(eidx, N_EXPERTS, dtype=jnp.float32) * w[..., None], 1)

    def expert(acc, xs):
        wg, wu, wd, g = xs
        y = (jax.nn.silu(t @ wg) * (t @ wu)) @ wd
        return acc + g[:, None] * y.astype(jnp.float32), None

    routed, _ = lax.scan(expert, jnp.zeros(t.shape, jnp.float32), (w_gate, w_up, w_down, gates.T))
    shared = (jax.nn.silu(t @ sh_gate) * (t @ sh_up)) @ sh_down
    return (routed.astype(t.dtype) + shared).reshape(b, s, d)


def setup_inputs(seed: int = 0) -> dict:
    key = jax.random.key(seed)
    ks = jax.random.split(key, 28)
    L, D, E, F, FS = DEPTH, D_MODEL, N_EXPERTS, EXPERT_FF, SHARED_FF

    def nrm(i, shape, scale):
        return scale * jax.random.normal(ks[i], shape, jnp.float32)

    return {
        "x": nrm(0, (BATCH, SEQ, D), 1.0),
        "c": nrm(1, (BATCH, D), 1.0),
        "w_ada": nrm(2, (L, D, 6 * D), 0.1 * D ** -0.5),
        "b_ada": nrm(3, (L, 6 * D), 0.02),
        "w_in": nrm(4, (L, D, IN_WIDTH), D ** -0.5),
        "mla_q_norm": 1.0 + nrm(5, (L, MLA_Q_RANK), 0.1),
        "mla_w_uq": nrm(6, (L, MLA_Q_RANK, MLA_HEADS * (MLA_NOPE + MLA_ROPE)), MLA_Q_RANK ** -0.5),
        "mla_kv_norm": 1.0 + nrm(7, (L, MLA_KV_RANK), 0.1),
        "mla_w_ukv": nrm(8, (L, MLA_KV_RANK, MLA_HEADS * (MLA_NOPE + MLA_V)), MLA_KV_RANK ** -0.5),
        "na_rpb": nrm(9, (L, NA_HEADS, 2 * NA_WIN_H - 1, 2 * NA_WIN_W - 1), 0.2),
        "diff_lq1": nrm(10, (L, DIFF_QK), 0.1),
        "diff_lk1": nrm(11, (L, DIFF_QK), 0.1),
        "diff_lq2": nrm(12, (L, DIFF_QK), 0.1),
        "diff_lk2": nrm(13, (L, DIFF_QK), 0.1),
        "diff_subln": 1.0 + nrm(14, (L, DIFF_V), 0.1),
        "w_o": nrm(15, (L, MIX_WIDTH, D), DN_BETA * MIX_WIDTH ** -0.5),
        "ln1_g": 1.0 + nrm(16, (L, D), 0.1),
        "ln1_b": nrm(17, (L, D), 0.02),
        "router_w": nrm(18, (L, D, E), D ** -0.5),
        "router_bias": nrm(19, (L, E), 0.01),
        "exp_w_gate": nrm(20, (L, E, D, F), D ** -0.5),
        "exp_w_up": nrm(21, (L, E, D, F), D ** -0.5),
        "exp_w_down": nrm(22, (L, E, F, D), DN_BETA * F ** -0.5),
        "sh_w_gate": nrm(23, (L, D, FS), D ** -0.5),
        "sh_w_up": nrm(24, (L, D, FS), D ** -0.5),
        "sh_w_down": nrm(25, (L, FS, D), DN_BETA * FS ** -0.5),
        "ln2_g": 1.0 + nrm(26, (L, D), 0.1),
        "ln2_b": nrm(27, (L, D), 0.02),
    }


def reference(x, c, w_ada, b_ada, w_in, mla_q_norm, mla_w_uq, mla_kv_norm, mla_w_ukv,
              na_rpb, diff_lq1, diff_lk1, diff_lq2, diff_lk2, diff_subln, w_o,
              ln1_g, ln1_b, router_w, router_bias, exp_w_gate, exp_w_up, exp_w_down,
              sh_w_gate, sh_w_up, sh_w_down, ln2_g, ln2_b):
    pos = jnp.arange(x.shape[1], dtype=jnp.int32)
    cond = jax.nn.silu(c)
    for i in range(DEPTH):
        lam_init = 0.8 - 0.6 * math.exp(-0.3 * i)
        ada = cond @ w_ada[i] + b_ada[i]
        sh_a, sc_a, g_a, sh_f, sc_f, g_f = [a[:, None, :] for a in jnp.split(ada, 6, -1)]

        h = _layernorm(x) * (1.0 + sc_a) + sh_a
        u = h @ w_in[i]
        ua, ub, uc = jnp.split(u, [IN_A, IN_A + IN_B], -1)
        mix = jnp.concatenate([
            _mla(ua, pos, mla_q_norm[i], mla_w_uq[i], mla_kv_norm[i], mla_w_ukv[i]),
            _neighbourhood_attn(ub, na_rpb[i]),
            _diff_attn(uc, pos, diff_lq1[i], diff_lk1[i], diff_lq2[i], diff_lk2[i],
                       diff_subln[i], lam_init),
        ], -1)
        x = _layernorm(DN_ALPHA * x + (1.0 + g_a) * (mix @ w_o[i]), ln1_g[i], ln1_b[i])

        h = _layernorm(x) * (1.0 + sc_f) + sh_f
        y = _moe(h, router_w[i], router_bias[i], exp_w_gate[i], exp_w_up[i], exp_w_down[i],
                 sh_w_gate[i], sh_w_up[i], sh_w_down[i])
        x = _layernorm(DN_ALPHA * x + (1.0 + g_f) * y, ln2_g[i], ln2_b[i])
    return x
```

```python
import functools
import math

import numpy as np
import jax
import jax.numpy as jnp
from jax import lax
from jax.experimental import pallas as pl
from jax.experimental.pallas import tpu as pltpu

F32 = jnp.float32
BF16 = jnp.bfloat16

GRID_W = 64
ROPE_THETA = 500000.0
MLA_HEADS, MLA_Q_RANK, MLA_KV_RANK, MLA_NOPE, MLA_ROPE, MLA_V = 8, 512, 256, 128, 64, 128
NA_HEADS, NA_DIM, NA_WIN_H, NA_WIN_W = 4, 128, 8, 16
DIFF_HEADS, DIFF_QK = 4, 64
DIFF_V = 2 * DIFF_QK
DIFF_ROT = DIFF_QK // 4
IN_A = MLA_Q_RANK + MLA_KV_RANK + MLA_ROPE
IN_B = 3 * NA_HEADS * NA_DIM
N_EXPERTS, TOP_K, N_GROUPS, TOPK_GROUPS = 64, 8, 8, 4
ROUTED_SCALE = 2.5
LN_EPS = 1e-5
RMS_EPS = 1e-6

LANES = 128
MLA_HEAD_PAD = 256
VMEM_LIMIT = 56 * 1024 * 1024
ROW_TILE = 512
ATTN_Q_TILE = 512
ATTN_KV_CHUNK = 1024
SOFTMAX_ROWS = 32
NA_PAIR = 2 * GRID_W
NA_KEYS = 10 * GRID_W
MOE_TILE = 256
COMBINE_TILE = 128
ROUTER_TILE = 512


def _params(sem, vmem=VMEM_LIMIT):
    return pltpu.CompilerParams(dimension_semantics=sem, vmem_limit_bytes=vmem)


def _sigmoid(x):
    return 1.0 / (1.0 + jnp.exp(-x))


def _silu(x):
    return x * _sigmoid(x)


def _ln(x):
    mu = jnp.mean(x, axis=-1, keepdims=True)
    xc = x - mu
    var = jnp.mean(xc * xc, axis=-1, keepdims=True)
    return xc * lax.rsqrt(var + LN_EPS)


def _rms(x, g):
    return x * lax.rsqrt(jnp.mean(x * x, axis=-1, keepdims=True) + RMS_EPS) * g


def _bdot(a, b):
    return jnp.dot(a, b, preferred_element_type=F32)


def _split(a):
    hi = a.astype(BF16)
    lo = (a - hi.astype(F32)).astype(BF16)
    return hi, lo


def _dot3(a, b):
    ah, al = _split(a)
    bh, bl = _split(b)
    return _bdot(ah, bh) + (_bdot(al, bh) + _bdot(ah, bl))


def _rope(x, c, s1, s2, half):
    width = x.shape[1]
    reps = width // c.shape[1]
    if reps > 1:
        c, s1, s2 = (jnp.concatenate([t] * reps, axis=1) for t in (c, s1, s2))
    return x * c + pltpu.roll(x, width - half, 1) * s1 + pltpu.roll(x, half, 1) * s2


def _ada_kernel(c_ref, w_ref, b_ref, o_ref):
    o_ref[0] = _dot3(_silu(c_ref[...]), w_ref[0]) + b_ref[0]


def _ada(c, w_ada, b_ada):
    depth, d, d6 = w_ada.shape
    batch = c.shape[0]
    rows = 8
    cp = jnp.zeros((rows, d), F32).at[:batch].set(c)
    tn = 512
    out = pl.pallas_call(
        _ada_kernel,
        grid=(depth, d6 // tn),
        in_specs=[pl.BlockSpec((rows, d), lambda l, n: (0, 0)),
                  pl.BlockSpec((1, d, tn), lambda l, n: (l, 0, n)),
                  pl.BlockSpec((1, 1, tn), lambda l, n: (l, 0, n))],
        out_specs=pl.BlockSpec((1, rows, tn), lambda l, n: (l, 0, n)),
        out_shape=jax.ShapeDtypeStruct((depth, rows, d6), F32),
        compiler_params=_params(("arbitrary", "arbitrary")),
        name="ada",
    )(cp, w_ada, b_ada.reshape(depth, 1, d6))
    return out[:, :batch]


def _inproj_kernel(x_ref, sc_ref, sh_ref, w_ref, *rest, rope_cols, q_cols, q_scale, half):
    o_ref = rest[-1]
    h = _ln(x_ref[0]) * (1.0 + sc_ref[0]) + sh_ref[0]
    acc = _bdot(h.astype(BF16), w_ref[...])
    if rope_cols:
        c_ref, s1_ref, s2_ref = rest[:3]
        r = _rope(acc[:, :rope_cols], c_ref[...], s1_ref[...], s2_ref[...], half)
        o_ref[0, :, :q_cols] = (r[:, :q_cols] * q_scale).astype(o_ref.dtype)
        o_ref[0, :, q_cols:rope_cols] = r[:, q_cols:].astype(o_ref.dtype)
        o_ref[0, :, rope_cols:] = acc[:, rope_cols:].astype(o_ref.dtype)
    else:
        o_ref[0] = acc.astype(o_ref.dtype)


def _inproj(x, sc, sh, w, out_dtype, name, tables=None, rope_cols=0, q_cols=0, q_scale=1.0, half=0):
    b, s, d = x.shape
    n = w.shape[1]
    tm = min(ROW_TILE, s)
    in_specs = [pl.BlockSpec((1, tm, d), lambda bi, m: (bi, m, 0)),
                pl.BlockSpec((1, 1, d), lambda bi, m: (bi, 0, 0)),
                pl.BlockSpec((1, 1, d), lambda bi, m: (bi, 0, 0)),
                pl.BlockSpec((d, n), lambda bi, m: (0, 0))]
    args = [x, sc, sh, w]
    if tables is not None:
        tw = tables[0].shape[1]
        in_specs += [pl.BlockSpec((tm, tw), lambda bi, m: (m, 0))] * 3
        args += list(tables)
    return pl.pallas_call(
        functools.partial(_inproj_kernel, rope_cols=rope_cols, q_cols=q_cols, q_scale=q_scale, half=half),
        grid=(b, s // tm),
        in_specs=in_specs,
        out_specs=pl.BlockSpec((1, tm, n), lambda bi, m: (bi, m, 0)),
        out_shape=jax.ShapeDtypeStruct((b, s, n), out_dtype),
        compiler_params=_params(("arbitrary", "arbitrary")),
        name=name,
    )(*args)


def _mla_proj_kernel(ua_ref, qn_ref, kvn_ref, wq_ref, wkv_ref, cq_ref, sq1_ref, sq2_ref,
                     ck_ref, sk1_ref, sk2_ref, q_ref, kt_ref, v_ref):
    ua = ua_ref[0]
    kv_lo = MLA_Q_RANK + MLA_KV_RANK
    cq = _rms(ua[:, :MLA_Q_RANK], qn_ref[...])
    q = _bdot(cq.astype(BF16), wq_ref[...])
    q = _rope(q, cq_ref[...], sq1_ref[...], sq2_ref[...], MLA_ROPE // 2)
    q_ref[0] = (q * ((MLA_NOPE + MLA_ROPE) ** -0.5)).astype(BF16)
    ckv = _rms(ua[:, MLA_Q_RANK:kv_lo], kvn_ref[...])
    kv = _bdot(ckv.astype(BF16), wkv_ref[...])
    nk = MLA_HEADS * MLA_NOPE
    v_ref[0] = kv[:, nk:].astype(BF16)
    kr = _rope(ua[:, kv_lo:], ck_ref[...], sk1_ref[...], sk2_ref[...], MLA_ROPE // 2)
    krt = kr.T.astype(BF16)
    for h in range(MLA_HEADS):
        kt_ref[0, h, :MLA_NOPE, :] = kv[:, h * MLA_NOPE:(h + 1) * MLA_NOPE].T.astype(BF16)
        kt_ref[0, h, MLA_NOPE:, :] = krt


def _mla_proj(ua, q_norm, kv_norm, wq, wkv, tq, tk):
    b, s, wa = ua.shape
    tm = min(ROW_TILE, s)
    hq = MLA_HEADS * MLA_HEAD_PAD
    hv = MLA_HEADS * MLA_V
    const = lambda bi, m: (0, 0)
    tab = lambda w: pl.BlockSpec((tm, w), lambda bi, m: (m, 0))
    return pl.pallas_call(
        _mla_proj_kernel,
        grid=(b, s // tm),
        in_specs=[pl.BlockSpec((1, tm, wa), lambda bi, m: (bi, m, 0)),
                  pl.BlockSpec((1, MLA_Q_RANK), const),
                  pl.BlockSpec((1, MLA_KV_RANK), const),
                  pl.BlockSpec(wq.shape, const),
                  pl.BlockSpec(wkv.shape, const),
                  tab(MLA_HEAD_PAD), tab(MLA_HEAD_PAD), tab(MLA_HEAD_PAD),
                  tab(LANES), tab(LANES), tab(LANES)],
        out_specs=[pl.BlockSpec((1, tm, hq), lambda bi, m: (bi, m, 0)),
                   pl.BlockSpec((1, MLA_HEADS, MLA_HEAD_PAD, tm), lambda bi, m: (bi, 0, 0, m)),
                   pl.BlockSpec((1, tm, hv), lambda bi, m: (bi, m, 0))],
        out_shape=[jax.ShapeDtypeStruct((b, s, hq), BF16),
                   jax.ShapeDtypeStruct((b, MLA_HEADS, MLA_HEAD_PAD, s), BF16),
                   jax.ShapeDtypeStruct((b, s, hv), BF16)],
        compiler_params=_params(("arbitrary", "arbitrary")),
        name="mla_proj",
    )(ua, q_norm.reshape(1, -1), kv_norm.reshape(1, -1), wq, wkv, *tq, *tk)


def _scores(q, kt_ref, s_scr):
    s = s_scr.shape[1]
    kc = min(ATTN_KV_CHUNK, s)
    for c in range(s // kc):
        s_scr[:, c * kc:(c + 1) * kc] = _bdot(q, kt_ref[0, 0, :, c * kc:(c + 1) * kc])


def _mla_attn_kernel(q_ref, kt_ref, v_ref, o_ref, s_scr, p_scr, linv_scr):
    _scores(q_ref[0], kt_ref, s_scr)
    rb = SOFTMAX_ROWS

    def body(r, carry):
        rows = pl.ds(pl.multiple_of(r * rb, rb), rb)
        sc = s_scr[rows, :]
        e = jnp.exp(sc - jnp.max(sc, axis=1, keepdims=True))
        p_scr[rows, :] = e.astype(BF16)
        linv_scr[rows, :] = 1.0 / jnp.sum(e, axis=1, keepdims=True)
        return carry

    lax.fori_loop(0, s_scr.shape[0] // rb, body, 0)
    o = _bdot(p_scr[...], v_ref[0])
    o_ref[0] = (o * linv_scr[...]).astype(BF16)


def _mla_attn(q, kt, v):
    b, s, _ = q.shape
    tq = min(ATTN_Q_TILE, s)
    return pl.pallas_call(
        _mla_attn_kernel,
        grid=(b, MLA_HEADS, s // tq),
        in_specs=[pl.BlockSpec((1, tq, MLA_HEAD_PAD), lambda bi, h, i: (bi, i, h)),
                  pl.BlockSpec((1, 1, MLA_HEAD_PAD, s), lambda bi, h, i: (bi, h, 0, 0)),
                  pl.BlockSpec((1, s, MLA_V), lambda bi, h, i: (bi, 0, h))],
        out_specs=pl.BlockSpec((1, tq, MLA_V), lambda bi, h, i: (bi, i, h)),
        out_shape=jax.ShapeDtypeStruct((b, s, MLA_HEADS * MLA_V), BF16),
        scratch_shapes=[pltpu.VMEM((tq, s), F32), pltpu.VMEM((tq, s), BF16), pltpu.VMEM((tq, 1), F32)],
        compiler_params=_params(("arbitrary", "arbitrary", "arbitrary")),
        name="mla_attn",
    )(q, kt, v)


def _diff_attn_kernel(q_ref, kt_ref, v_ref, lq1_ref, lk1_ref, lq2_ref, lk2_ref, sub_ref, o_ref,
                      s1_scr, s2_scr, p_scr, *, lam_init):
    q = q_ref[0]
    lane = lax.broadcasted_iota(jnp.int32, q.shape, 1)
    zero = jnp.zeros_like(q)
    _scores(jnp.where(lane < DIFF_QK, q, zero), kt_ref, s1_scr)
    _scores(jnp.where(lane >= DIFF_QK, q, zero), kt_ref, s2_scr)
    lam = (jnp.exp(jnp.sum(lq1_ref[...] * lk1_ref[...], axis=1, keepdims=True))
           - jnp.exp(jnp.sum(lq2_ref[...] * lk2_ref[...], axis=1, keepdims=True)) + lam_init)
    rb = SOFTMAX_ROWS

    def body(r, carry):
        rows = pl.ds(pl.multiple_of(r * rb, rb), rb)
        a = s1_scr[rows, :]
        e1 = jnp.exp(a - jnp.max(a, axis=1, keepdims=True))
        a1 = e1 * (1.0 / jnp.sum(e1, axis=1, keepdims=True))
        c = s2_scr[rows, :]
        e2 = jnp.exp(c - jnp.max(c, axis=1, keepdims=True))
        a2 = e2 * (1.0 / jnp.sum(e2, axis=1, keepdims=True))
        p_scr[rows, :] = (a1 - lam * a2).astype(BF16)
        return carry

    lax.fori_loop(0, s1_scr.shape[0] // rb, body, 0)
    o = _bdot(p_scr[...], v_ref[0])
    o_ref[0] = (_rms(o, sub_ref[...]) * (1.0 - lam_init)).astype(BF16)


def _diff_attn(uc, kt, lq1, lk1, lq2, lk2, subln, lam_init):
    b, s, _ = uc.shape
    tq = min(ATTN_Q_TILE, s)
    vec = lambda n: pl.BlockSpec((1, n), lambda bi, h, i: (0, 0))
    v_block0 = 2 * DIFF_HEADS
    return pl.pallas_call(
        functools.partial(_diff_attn_kernel, lam_init=lam_init),
        grid=(b, DIFF_HEADS, s // tq),
        in_specs=[pl.BlockSpec((1, tq, DIFF_V), lambda bi, h, i: (bi, i, h)),
                  pl.BlockSpec((1, 1, 2 * DIFF_QK, s), lambda bi, h, i: (bi, h, 0, 0)),
                  pl.BlockSpec((1, s, DIFF_V), lambda bi, h, i: (bi, 0, v_block0 + h)),
                  vec(DIFF_QK), vec(DIFF_QK), vec(DIFF_QK), vec(DIFF_QK), vec(DIFF_V)],
        out_specs=pl.BlockSpec((1, tq, DIFF_V), lambda bi, h, i: (bi, i, h)),
        out_shape=jax.ShapeDtypeStruct((b, s, DIFF_HEADS * DIFF_V), BF16),
        scratch_shapes=[pltpu.VMEM((tq, s), F32), pltpu.VMEM((tq, s), F32), pltpu.VMEM((tq, s), BF16)],
        compiler_params=_params(("arbitrary", "arbitrary", "arbitrary")),
        name="diff_attn",
    )(uc, kt, uc, lq1.reshape(1, -1), lk1.reshape(1, -1), lq2.reshape(1, -1), lk2.reshape(1, -1),
      subln.reshape(1, -1))


def _kt_kernel(k_ref, o_ref):
    o_ref[0, 0] = k_ref[0].astype(F32).T.astype(BF16)


def _head_transpose(u, first_block, heads, width):
    b, s, _ = u.shape
    tm = min(ROW_TILE, s)
    return pl.pallas_call(
        _kt_kernel,
        grid=(b, heads, s // tm),
        in_specs=[pl.BlockSpec((1, tm, width), lambda bi, h, m: (bi, m, first_block + h))],
        out_specs=pl.BlockSpec((1, 1, width, tm), lambda bi, h, m: (bi, h, 0, m)),
        out_shape=jax.ShapeDtypeStruct((b, heads, width, s), BF16),
        compiler_params=_params(("arbitrary", "arbitrary", "arbitrary")),
        name="head_transpose",
    )(u)


def _na_plan(rows):
    kh = min(NA_WIN_H, rows)
    key_rows = NA_KEYS // GRID_W
    r0 = lambda r: min(max(r - kh // 2, 0), rows - kh)
    starts, cases, sigs = [], [], []
    for p in range(rows // 2):
        sr = min(r0(2 * p), rows - key_rows)
        sig = (r0(2 * p) - 2 * p, r0(2 * p + 1) - 2 * p - 1, sr - 2 * p)
        if sig not in sigs:
            sigs.append(sig)
        starts.append(sr)
        cases.append(sigs.index(sig))
    return np.array(starts, np.int32), np.array(cases, np.int32), sigs, kh


def _na_bias(rpb, rows):
    _, _, sigs, kh = _na_plan(rows)
    j = np.arange(NA_PAIR)
    m = np.arange(NA_KEYS)
    qrow_rel, qcol = j // GRID_W, j % GRID_W
    krow_rel, kcol = m // GRID_W, m % GRID_W
    c0 = np.clip(qcol - NA_WIN_W // 2, 0, GRID_W - NA_WIN_W)
    col_ok = (kcol[None, :] >= c0[:, None]) & (kcol[None, :] < c0[:, None] + NA_WIN_W)
    dc = np.clip(kcol[None, :] - qcol[:, None], -(NA_WIN_W - 1), NA_WIN_W - 1) + (NA_WIN_W - 1)
    tabs = []
    for (d0, d1, dsr) in sigs:
        r0_rel = np.where(qrow_rel == 0, d0, d1 + 1)
        krel = dsr + krow_rel
        row_ok = (krel[None, :] >= r0_rel[:, None]) & (krel[None, :] < r0_rel[:, None] + kh)
        dr = krel[None, :] - qrow_rel[:, None] + (NA_WIN_H - 1)
        ok = row_ok & col_ok
        bias = rpb[:, np.clip(dr, 0, 2 * NA_WIN_H - 2), dc]
        tabs.append(jnp.where(ok[None], bias, -jnp.inf))
    return jnp.stack(tabs, axis=1).astype(F32)


def _na_kernel(start_ref, case_ref, q_ref, k_ref, v_ref, bias_ref, o_ref):
    def body(p, carry):
        qs = pl.ds(pl.multiple_of(p * NA_PAIR, NA_PAIR), NA_PAIR)
        ks = pl.ds(pl.multiple_of(start_ref[p] * GRID_W, LANES), NA_KEYS)
        q = q_ref[0, qs, :]
        sc = lax.dot_general(q, k_ref[0, ks, :], (((1,), (1,)), ((), ())), preferred_element_type=F32)
        sc = sc * (NA_DIM ** -0.5) + bias_ref[0, case_ref[p]]
        e = jnp.exp(sc - jnp.max(sc, axis=1, keepdims=True))
        pr = e / jnp.sum(e, axis=1, keepdims=True)
        o_ref[0, qs, :] = _bdot(pr.astype(BF16), v_ref[0, ks, :]).astype(BF16)
        return carry

    lax.fori_loop(0, q_ref.shape[1] // NA_PAIR, body, 0)


def _na_attn(ub, bias):
    b, s, _ = ub.shape
    starts, cases, sigs, _ = _na_plan(s // GRID_W)
    blk = lambda off: pl.BlockSpec((1, s, NA_DIM), lambda bi, h, st, cs: (bi, 0, off + h))
    return pl.pallas_call(
        _na_kernel,
        grid_spec=pltpu.PrefetchScalarGridSpec(
            num_scalar_prefetch=2,
            grid=(b, NA_HEADS),
            in_specs=[blk(0), blk(NA_HEADS), blk(2 * NA_HEADS),
                      pl.BlockSpec((1, len(sigs), NA_PAIR, NA_KEYS), lambda bi, h, st, cs: (h, 0, 0, 0))],
            out_specs=pl.BlockSpec((1, s, NA_DIM), lambda bi, h, st, cs: (bi, 0, h)),
        ),
        out_shape=jax.ShapeDtypeStruct((b, s, NA_HEADS * NA_DIM), BF16),
        compiler_params=_params(("arbitrary", "arbitrary")),
        name="na_attn",
    )(jnp.asarray(starts), jnp.asarray(cases), ub, ub, ub, bias)


def _oproj_kernel(oa_ref, ob_ref, oc_ref, x_ref, ga_ref, wo_ref, g1_ref, b1_ref, scf_ref, shf_ref, rw_ref,
                  x1_ref, h2_ref, lg_ref, *, alpha):
    na, nb = oa_ref.shape[2], ob_ref.shape[2]
    acc = (_bdot(oa_ref[0], wo_ref[:na, :]) + _bdot(ob_ref[0], wo_ref[na:na + nb, :])
           + _bdot(oc_ref[0], wo_ref[na + nb:, :]))
    x1 = _ln(alpha * x_ref[0] + (1.0 + ga_ref[0]) * acc) * g1_ref[...] + b1_ref[...]
    x1_ref[0] = x1
    h2 = _ln(x1) * (1.0 + scf_ref[0]) + shf_ref[0]
    h2_ref[0] = h2
    lg_ref[0] = _dot3(h2, rw_ref[...])


def _oproj(oa, ob, oc, x, ga, wo, g1, b1, scf, shf, rw, alpha):
    b, s, d = x.shape
    tm = min(ROW_TILE // 2, s)
    row = lambda w: pl.BlockSpec((1, tm, w), lambda bi, m: (bi, m, 0))
    per_b = pl.BlockSpec((1, 1, d), lambda bi, m: (bi, 0, 0))
    const = lambda shape: pl.BlockSpec(shape, lambda bi, m: (0, 0))
    return pl.pallas_call(
        functools.partial(_oproj_kernel, alpha=alpha),
        grid=(b, s // tm),
        in_specs=[row(oa.shape[2]), row(ob.shape[2]), row(oc.shape[2]), row(d), per_b,
                  const(wo.shape), const((1, d)), const((1, d)), per_b, per_b, const(rw.shape)],
        out_specs=[row(d), row(d), row(LANES)],
        out_shape=[jax.ShapeDtypeStruct((b, s, d), F32), jax.ShapeDtypeStruct((b, s, d), F32),
                   jax.ShapeDtypeStruct((b, s, LANES), F32)],
        compiler_params=_params(("arbitrary", "arbitrary")),
        name="oproj",
    )(oa, ob, oc, x, ga, wo, g1.reshape(1, d), b1.reshape(1, d), scf, shf, rw)


def _first_max(cur, idx, axis, size):
    m = jnp.max(cur, axis=axis, keepdims=True)
    first = jnp.min(jnp.where(cur == m, idx, size), axis=axis, keepdims=True)
    return idx == first, m


def _router_kernel(lg_ref, rb_ref, e_ref, rank_ref, wt_ref, cnt_ref, carry_scr):
    t = pl.program_id(0)
    tn = lg_ref.shape[0]
    gsz = N_EXPERTS // N_GROUPS
    neg = -jnp.inf

    @pl.when(t == 0)
    def _():
        carry_scr[...] = jnp.zeros_like(carry_scr)

    scores = _sigmoid(lg_ref[...].T[:N_EXPERTS, :])
    biased = scores + rb_ref[...]
    b3 = biased.reshape(N_GROUPS, gsz, tn)
    i3 = lax.broadcasted_iota(jnp.int32, b3.shape, 1)
    pick1, m1 = _first_max(b3, i3, 1, gsz)
    m2 = jnp.max(jnp.where(pick1, neg, b3), axis=1, keepdims=True)
    gs = (m1 + m2).reshape(N_GROUPS, tn)
    gi = lax.broadcasted_iota(jnp.int32, gs.shape, 0)
    gsel = jnp.zeros(gs.shape, F32)
    for _ in range(TOPK_GROUPS):
        pick, _m = _first_max(gs, gi, 0, N_GROUPS)
        gsel = jnp.where(pick, 1.0, gsel)
        gs = jnp.where(pick, neg, gs)
    emask = jnp.broadcast_to(gsel.reshape(N_GROUPS, 1, tn), b3.shape).reshape(N_EXPERTS, tn)
    cur = jnp.where(emask > 0.0, biased, neg)
    ei = lax.broadcasted_iota(jnp.int32, cur.shape, 0)
    self_ = jnp.zeros(cur.shape, F32)
    for _ in range(TOP_K):
        pick, _m = _first_max(cur, ei, 0, N_EXPERTS)
        self_ = jnp.where(pick, 1.0, self_)
        cur = jnp.where(pick, neg, cur)
    sel = self_ > 0.0
    w = jnp.where(sel, scores, 0.0)
    gate = w / jnp.sum(w, axis=0, keepdims=True) * ROUTED_SCALE
    selb = self_.astype(BF16)
    lower = (lax.broadcasted_iota(jnp.int32, (N_EXPERTS, N_EXPERTS), 1)
             < lax.broadcasted_iota(jnp.int32, (N_EXPERTS, N_EXPERTS), 0)).astype(BF16)
    kidx = _bdot(lower, selb)
    upper = (lax.broadcasted_iota(jnp.int32, (tn, tn), 0)
             <= lax.broadcasted_iota(jnp.int32, (tn, tn), 1)).astype(BF16)
    rank = _bdot(selb, upper) + carry_scr[...] - 1.0
    carry_scr[...] = carry_scr[...] + jnp.sum(self_, axis=1, keepdims=True)
    cnt_ref[...] = jnp.broadcast_to(carry_scr[...], cnt_ref.shape)
    eif = ei.astype(F32)
    e_rows, r_rows, w_rows = [], [], []
    for k in range(TOP_K):
        mk = sel & (kidx == float(k))
        e_rows.append(jnp.sum(jnp.where(mk, eif, 0.0), axis=0, keepdims=True))
        r_rows.append(jnp.sum(jnp.where(mk, rank, 0.0), axis=0, keepdims=True))
        w_rows.append(jnp.sum(jnp.where(mk, gate, 0.0), axis=0, keepdims=True))
    e_ref[...] = jnp.concatenate(e_rows, axis=0).astype(jnp.int32)
    rank_ref[...] = jnp.concatenate(r_rows, axis=0).astype(jnp.int32)
    wk = jnp.concatenate(w_rows + [jnp.zeros((LANES - TOP_K, tn), F32)], axis=0)
    wt_ref[...] = wk.T


def _router(logits, router_bias):
    n = logits.shape[0]
    tn = min(ROUTER_TILE, n)
    return pl.pallas_call(
        _router_kernel,
        grid=(n // tn,),
        in_specs=[pl.BlockSpec((tn, LANES), lambda t: (t, 0)),
                  pl.BlockSpec((N_EXPERTS, 1), lambda t: (0, 0))],
        out_specs=[pl.BlockSpec((TOP_K, tn), lambda t: (0, t)),
                   pl.BlockSpec((TOP_K, tn), lambda t: (0, t)),
                   pl.BlockSpec((tn, LANES), lambda t: (t, 0)),
                   pl.BlockSpec((N_EXPERTS, LANES), lambda t: (0, 0))],
        out_shape=[jax.ShapeDtypeStruct((TOP_K, n), jnp.int32),
                   jax.ShapeDtypeStruct((TOP_K, n), jnp.int32),
                   jax.ShapeDtypeStruct((n, LANES), F32),
                   jax.ShapeDtypeStruct((N_EXPERTS, LANES), F32)],
        scratch_shapes=[pltpu.VMEM((N_EXPERTS, 1), F32)],
        compiler_params=_params(("arbitrary",)),
        name="router",
    )(logits, router_bias.reshape(N_EXPERTS, 1))


def _moe_plan(eidx, rank, counts, n):
    tm = MOE_TILE
    n_tiles = TOP_K * n // tm + N_EXPERTS
    ntile = (counts + tm - 1) // tm
    tile_end = jnp.cumsum(ntile)
    tile_start = tile_end - ntile
    n_used = tile_end[-1]
    ti = jnp.minimum(jnp.arange(n_tiles, dtype=jnp.int32), n_used - 1)
    te = jnp.minimum(jnp.searchsorted(tile_end, ti, side="right"), N_EXPERTS - 1).astype(jnp.int32)
    first = (jnp.arange(n_tiles, dtype=jnp.int32) == tile_start[te]).astype(jnp.int32)
    nvalid = jnp.clip(counts[te] - (ti - tile_start[te]) * tm, 0, tm).astype(jnp.int32)
    slot = (tile_start[eidx] * tm + rank).reshape(-1)
    tok = jnp.broadcast_to(jnp.arange(n, dtype=jnp.int32)[None, :], (TOP_K, n))
    row = tok + n * jnp.arange(TOP_K, dtype=jnp.int32)[:, None]
    zeros = jnp.zeros((n_tiles * tm,), jnp.int32)
    src = zeros.at[slot].set(tok.reshape(-1))
    dst = zeros.at[slot].set(row.reshape(-1))
    return (te, first, nvalid, n_used.reshape(1).astype(jnp.int32),
            src.reshape(n_tiles, 1, tm), dst.reshape(n_tiles, 1, tm))


def _moe_kernel(te_ref, first_ref, nvalid_ref, nused_ref, src_cur, src_nxt, dst_cur, h_hbm, wg_ref, wu_ref,
                wd_ref, y_hbm, xbuf, ybuf, wgb, wub, wdb, gsem, ssem):
    i = pl.program_id(0)
    tm = xbuf.shape[1]
    n_used = nused_ref[0]
    nv = nvalid_ref[i]
    slot = i % 2

    def gather(src_ref, sl):
        for r in range(tm):
            pltpu.make_async_copy(h_hbm.at[pl.ds(src_ref[0, 0, r], 1)], xbuf.at[sl, pl.ds(r, 1)],
                                  gsem.at[sl]).start()

    def gather_wait(sl):
        pltpu.make_async_copy(h_hbm.at[pl.ds(0, tm)], xbuf.at[sl], gsem.at[sl]).wait()

    def scatter_row(r):
        pltpu.make_async_copy(ybuf.at[slot, pl.ds(r, 1)], y_hbm.at[pl.ds(dst_cur[0, 0, r], 1)],
                              ssem.at[slot]).start()

    def scatter_wait(sl, rows):
        @pl.when(rows == tm)
        def _():
            pltpu.make_async_copy(ybuf.at[sl], y_hbm.at[pl.ds(0, tm)], ssem.at[sl]).wait()

        @pl.when(rows < tm)
        def _():
            def row_wait(r, carry):
                pltpu.make_async_copy(ybuf.at[sl, pl.ds(0, 1)], y_hbm.at[pl.ds(0, 1)], ssem.at[sl]).wait()
                return carry

            lax.fori_loop(0, rows, row_wait, 0)

    @pl.when(i < n_used)
    def _():
        @pl.when(i == 0)
        def _():
            gather(src_cur, 0)

        gather(src_nxt, 1 - slot)
        gather_wait(slot)

        @pl.when(first_ref[i] == 1)
        def _():
            wgb[...] = wg_ref[0, 0].astype(BF16)
            wub[...] = wu_ref[0, 0].astype(BF16)
            wdb[...] = wd_ref[0, 0].astype(BF16)

        x = xbuf[slot].astype(BF16)
        act = _silu(_bdot(x, wgb[...])) * _bdot(x, wub[...])
        y = _bdot(act.astype(BF16), wdb[...])

        @pl.when(i >= 2)
        def _():
            scatter_wait(slot, nvalid_ref[i - 2])

        ybuf[slot] = y

        @pl.when(nv == tm)
        def _():
            for r in range(tm):
                scatter_row(r)

        @pl.when(nv < tm)
        def _():
            def row_body(r, carry):
                scatter_row(r)
                return carry

            lax.fori_loop(0, nv, row_body, 0)

        @pl.when(i == n_used - 1)
        def _():
            gather_wait(1 - slot)
            scatter_wait(slot, nv)

            @pl.when(i >= 1)
            def _():
                scatter_wait(1 - slot, nvalid_ref[i - 1])


def _moe(h2, plan, w_gate, w_up, w_down, layer):
    n, d = h2.shape
    te, first, nvalid, n_used, src, dst = plan
    n_tiles, _, tm = src.shape
    f = w_gate.shape[3]
    idx_blk = lambda fn: pl.BlockSpec((1, 1, tm), fn, memory_space=pltpu.SMEM)
    wspec = lambda shape: pl.BlockSpec((1, 1) + shape, lambda i, te_r, *_: (layer, te_r[i], 0, 0))
    return pl.pallas_call(
        _moe_kernel,
        grid_spec=pltpu.PrefetchScalarGridSpec(
            num_scalar_prefetch=4,
            grid=(n_tiles,),
            in_specs=[idx_blk(lambda i, *_: (i, 0, 0)),
                      idx_blk(lambda i, *_: (jnp.minimum(i + 1, n_tiles - 1), 0, 0)),
                      idx_blk(lambda i, *_: (i, 0, 0)),
                      pl.BlockSpec(memory_space=pl.ANY),
                      wspec((d, f)), wspec((d, f)), wspec((f, d))],
            out_specs=pl.BlockSpec(memory_space=pl.ANY),
            scratch_shapes=[pltpu.VMEM((2, tm, d), F32), pltpu.VMEM((2, tm, d), F32),
                            pltpu.VMEM((d, f), BF16), pltpu.VMEM((d, f), BF16), pltpu.VMEM((f, d), BF16),
                            pltpu.SemaphoreType.DMA((2,)), pltpu.SemaphoreType.DMA((2,))],
        ),
        out_shape=jax.ShapeDtypeStruct((TOP_K * n, d), F32),
        compiler_params=_params(("arbitrary",)),
        name="moe_experts",
    )(te, first, nvalid, n_used, src, src, dst, h2, w_gate, w_up, w_down)


def _combine_kernel(*refs, alpha):
    y_refs = refs[:TOP_K]
    wt_ref, h_ref, x_ref, gf_ref, sg_ref, su_ref, sd_ref, g2_ref, b2_ref, o_ref = refs[TOP_K:]
    wt = wt_ref[...]
    routed = wt[:, 0:1] * y_refs[0][...]
    for k in range(1, TOP_K):
        routed = routed + wt[:, k:k + 1] * y_refs[k][...]
    hb = h_ref[...].astype(BF16)
    act = _silu(_bdot(hb, sg_ref[...])) * _bdot(hb, su_ref[...])
    y = routed + _bdot(act.astype(BF16), sd_ref[...])
    o_ref[...] = _ln(alpha * x_ref[...] + (1.0 + gf_ref[0]) * y) * g2_ref[...] + b2_ref[...]


def _combine(y_tok, wt, h2, x1, gf, sg, su, sd, g2, b2, alpha, seq):
    n, d = h2.shape
    tm = min(COMBINE_TILE, seq)
    nblk = n // tm
    row = pl.BlockSpec((tm, d), lambda i: (i, 0))
    const = lambda shape: pl.BlockSpec(shape, lambda i: (0, 0))
    y_specs = [pl.BlockSpec((tm, d), (lambda i, k=k: (k * nblk + i, 0))) for k in range(TOP_K)]
    return pl.pallas_call(
        functools.partial(_combine_kernel, alpha=alpha),
        grid=(nblk,),
        in_specs=y_specs + [pl.BlockSpec((tm, LANES), lambda i: (i, 0)), row, row,
                            pl.BlockSpec((1, 1, d), lambda i: (i // (seq // tm), 0, 0)),
                            const(sg.shape), const(su.shape), const(sd.shape), const((1, d)), const((1, d))],
        out_specs=row,
        out_shape=jax.ShapeDtypeStruct((n, d), F32),
        compiler_params=_params(("arbitrary",)),
        name="combine",
    )(*([y_tok] * TOP_K), wt, h2, x1, gf, sg, su, sd, g2.reshape(1, d), b2.reshape(1, d))


def _rope_tables(s, half, width, offset, period=None):
    inv = ROPE_THETA ** (-jnp.arange(half, dtype=F32) / half)
    ang = jnp.arange(s, dtype=jnp.int32).astype(F32)[:, None] * inv[None, :]
    cos, sin = jnp.cos(ang), jnp.sin(ang)
    period = period or width
    c = jnp.ones((s, period), F32).at[:, offset:offset + half].set(cos)
    c = c.at[:, offset + half:offset + 2 * half].set(cos)
    s1 = jnp.zeros((s, period), F32).at[:, offset:offset + half].set(-sin)
    s2 = jnp.zeros((s, period), F32).at[:, offset + half:offset + 2 * half].set(sin)
    reps = width // period
    return tuple(jnp.tile(t, (1, reps)) for t in (c, s1, s2))


def kernel(x, c, w_ada, b_ada, w_in, mla_q_norm, mla_w_uq, mla_kv_norm, mla_w_ukv, na_rpb, diff_lq1, diff_lk1,
           diff_lq2, diff_lk2, diff_subln, w_o, ln1_g, ln1_b, router_w, router_bias, exp_w_gate, exp_w_up,
           exp_w_down, sh_w_gate, sh_w_up, sh_w_down, ln2_g, ln2_b):
    b, s, d = x.shape
    depth = w_ada.shape[0]
    n = b * s
    alpha = (2 * depth) ** 0.25
    ada = _ada(c, w_ada, b_ada)
    tab_diff = _rope_tables(s, DIFF_ROT // 2, LANES, 0, period=DIFF_QK)
    tab_q = _rope_tables(s, MLA_ROPE // 2, MLA_HEAD_PAD, MLA_NOPE)
    tab_k = _rope_tables(s, MLA_ROPE // 2, LANES, 0)
    nq_diff = DIFF_HEADS * 2 * DIFF_QK
    a_pad = LANES - MLA_ROPE

    for i in range(depth):
        lam_init = 0.8 - 0.6 * math.exp(-0.3 * i)
        sh_a, sc_a, g_a, sh_f, sc_f, g_f = [a[:, None, :] for a in jnp.split(ada[i], 6, -1)]
        wi = w_in[i]
        w_a = jnp.pad(wi[:, :IN_A], ((0, 0), (0, a_pad))).astype(BF16)
        w_b = wi[:, IN_A:IN_A + IN_B].astype(BF16)
        w_c = wi[:, IN_A + IN_B:].astype(BF16)
        wq = jnp.pad(mla_w_uq[i].reshape(MLA_Q_RANK, MLA_HEADS, MLA_NOPE + MLA_ROPE),
                     ((0, 0), (0, 0), (0, MLA_HEAD_PAD - MLA_NOPE - MLA_ROPE)))
        wq = wq.reshape(MLA_Q_RANK, MLA_HEADS * MLA_HEAD_PAD).astype(BF16)
        wkv3 = mla_w_ukv[i].reshape(MLA_KV_RANK, MLA_HEADS, MLA_NOPE + MLA_V)
        wkv = jnp.concatenate([wkv3[:, :, :MLA_NOPE].reshape(MLA_KV_RANK, -1),
                               wkv3[:, :, MLA_NOPE:].reshape(MLA_KV_RANK, -1)], axis=1).astype(BF16)
        rw = jnp.pad(router_w[i], ((0, 0), (0, LANES - N_EXPERTS)))

        ua = _inproj(x, sc_a, sh_a, w_a, F32, "inproj_mla")
        ub = _inproj(x, sc_a, sh_a, w_b, BF16, "inproj_na")
        uc = _inproj(x, sc_a, sh_a, w_c, BF16, "inproj_diff", tables=tab_diff, rope_cols=2 * nq_diff,
                     q_cols=nq_diff, q_scale=DIFF_QK ** -0.5, half=DIFF_ROT // 2)
        q_a, kt_a, v_a = _mla_proj(ua, mla_q_norm[i], mla_kv_norm[i], wq, wkv, tab_q, tab_k)
        o_a = _mla_attn(q_a, kt_a, v_a)
        o_b = _na_attn(ub, _na_bias(na_rpb[i], s // GRID_W))
        kt_c = _head_transpose(uc, DIFF_HEADS, DIFF_HEADS, 2 * DIFF_QK)
        o_c = _diff_attn(uc, kt_c, diff_lq1[i], diff_lk1[i], diff_lq2[i], diff_lk2[i], diff_subln[i], lam_init)
        x1, h2, logits = _oproj(o_a, o_b, o_c, x, g_a, w_o[i].astype(BF16), ln1_g[i], ln1_b[i], sc_f, sh_f,
                                rw, alpha)

        h2 = h2.reshape(n, d)
        eidx, rank, wt, cnt = _router(logits.reshape(n, LANES), router_bias[i])
        plan = _moe_plan(eidx, rank, cnt[:, 0].astype(jnp.int32), n)
        y_tok = _moe(h2, plan, exp_w_gate, exp_w_up, exp_w_down, i)
        x = _combine(y_tok, wt, h2, x1.reshape(n, d), g_f, sh_w_gate[i].astype(BF16), sh_w_up[i].astype(BF16),
                     sh_w_down[i].astype(BF16), ln2_g[i], ln2_b[i], alpha, s).reshape(b, s, d)
    return x
```

```python
import functools
import math

import numpy as np
import jax
import jax.numpy as jnp
from jax import lax
from jax.experimental import pallas as pl
from jax.experimental.pallas import tpu as pltpu

F32 = jnp.float32
BF16 = jnp.bfloat16

GRID_W = 64
ROPE_THETA = 500000.0
MLA_HEADS, MLA_Q_RANK, MLA_KV_RANK, MLA_NOPE, MLA_ROPE, MLA_V = 8, 512, 256, 128, 64, 128
NA_HEADS, NA_DIM, NA_WIN_H, NA_WIN_W = 4, 128, 8, 16
DIFF_HEADS, DIFF_QK = 4, 64
DIFF_V = 2 * DIFF_QK
DIFF_ROT = DIFF_QK // 4
IN_A = MLA_Q_RANK + MLA_KV_RANK + MLA_ROPE
IN_B = 3 * NA_HEADS * NA_DIM
N_EXPERTS, TOP_K, N_GROUPS, TOPK_GROUPS = 64, 8, 8, 4
ROUTED_SCALE = 2.5
LN_EPS = 1e-5
RMS_EPS = 1e-6

LANES = 128
MLA_HEAD_PAD = 256
VMEM_LIMIT = 56 * 1024 * 1024
ROW_TILE = 512
ATTN_Q_TILE = 512
ATTN_KV_CHUNK = 512
NA_PAIR = 2 * GRID_W
NA_KEYS = 10 * GRID_W
MOE_TILE = 256
COMBINE_TILE = 128
ROUTER_TILE = 512


def _params(sem, vmem=VMEM_LIMIT):
    return pltpu.CompilerParams(dimension_semantics=sem, vmem_limit_bytes=vmem)


def _sigmoid(x):
    return 1.0 / (1.0 + jnp.exp(-x))


def _silu(x):
    return x * _sigmoid(x)


def _ln(x):
    mu = jnp.mean(x, axis=-1, keepdims=True)
    xc = x - mu
    var = jnp.mean(xc * xc, axis=-1, keepdims=True)
    return xc * lax.rsqrt(var + LN_EPS)


def _rms(x, g):
    return x * lax.rsqrt(jnp.mean(x * x, axis=-1, keepdims=True) + RMS_EPS) * g


def _bdot(a, b):
    return jnp.dot(a, b, preferred_element_type=F32)


def _split(a):
    hi = a.astype(BF16)
    lo = (a - hi.astype(F32)).astype(BF16)
    return hi, lo


def _dot3(a, b):
    ah, al = _split(a)
    bh, bl = _split(b)
    return _bdot(ah, bh) + (_bdot(al, bh) + _bdot(ah, bl))


def _rope(x, c, s1, s2, half):
    width = x.shape[1]
    reps = width // c.shape[1]
    if reps > 1:
        c, s1, s2 = (jnp.concatenate([t] * reps, axis=1) for t in (c, s1, s2))
    return x * c + pltpu.roll(x, width - half, 1) * s1 + pltpu.roll(x, half, 1) * s2


def _ada_kernel(c_ref, w_ref, b_ref, o_ref):
    o_ref[0] = _dot3(_silu(c_ref[...]), w_ref[0]) + b_ref[0]


def _ada(c, w_ada, b_ada):
    depth, d, d6 = w_ada.shape
    batch = c.shape[0]
    rows = 8
    cp = jnp.zeros((rows, d), F32).at[:batch].set(c)
    tn = 512
    out = pl.pallas_call(
        _ada_kernel,
        grid=(depth, d6 // tn),
        in_specs=[pl.BlockSpec((rows, d), lambda l, n: (0, 0)),
                  pl.BlockSpec((1, d, tn), lambda l, n: (l, 0, n)),
                  pl.BlockSpec((1, 1, tn), lambda l, n: (l, 0, n))],
        out_specs=pl.BlockSpec((1, rows, tn), lambda l, n: (l, 0, n)),
        out_shape=jax.ShapeDtypeStruct((depth, rows, d6), F32),
        compiler_params=_params(("arbitrary", "arbitrary")),
        name="ada",
    )(cp, w_ada, b_ada.reshape(depth, 1, d6))
    return out[:, :batch]


def _inproj_kernel(x_ref, sc_ref, sh_ref, w_ref, *rest, rope_cols, q_cols, q_scale, half):
    o_ref = rest[-1]
    h = _ln(x_ref[0]) * (1.0 + sc_ref[0]) + sh_ref[0]
    acc = _bdot(h.astype(BF16), w_ref[...])
    if rope_cols:
        c_ref, s1_ref, s2_ref = rest[:3]
        r = _rope(acc[:, :rope_cols], c_ref[...], s1_ref[...], s2_ref[...], half)
        o_ref[0, :, :q_cols] = (r[:, :q_cols] * q_scale).astype(o_ref.dtype)
        o_ref[0, :, q_cols:rope_cols] = r[:, q_cols:].astype(o_ref.dtype)
        o_ref[0, :, rope_cols:] = acc[:, rope_cols:].astype(o_ref.dtype)
    else:
        o_ref[0] = acc.astype(o_ref.dtype)


def _inproj(x, sc, sh, w, out_dtype, name, tables=None, rope_cols=0, q_cols=0, q_scale=1.0, half=0):
    b, s, d = x.shape
    n = w.shape[1]
    tm = min(ROW_TILE, s)
    in_specs = [pl.BlockSpec((1, tm, d), lambda bi, m: (bi, m, 0)),
                pl.BlockSpec((1, 1, d), lambda bi, m: (bi, 0, 0)),
                pl.BlockSpec((1, 1, d), lambda bi, m: (bi, 0, 0)),
                pl.BlockSpec((d, n), lambda bi, m: (0, 0))]
    args = [x, sc, sh, w]
    if tables is not None:
        tw = tables[0].shape[1]
        in_specs += [pl.BlockSpec((tm, tw), lambda bi, m: (m, 0))] * 3
        args += list(tables)
    return pl.pallas_call(
        functools.partial(_inproj_kernel, rope_cols=rope_cols, q_cols=q_cols, q_scale=q_scale, half=half),
        grid=(b, s // tm),
        in_specs=in_specs,
        out_specs=pl.BlockSpec((1, tm, n), lambda bi, m: (bi, m, 0)),
        out_shape=jax.ShapeDtypeStruct((b, s, n), out_dtype),
        compiler_params=_params(("arbitrary", "arbitrary")),
        name=name,
    )(*args)


def _mla_proj_kernel(ua_ref, qn_ref, kvn_ref, wq_ref, wkv_ref, cq_ref, sq1_ref, sq2_ref,
                     ck_ref, sk1_ref, sk2_ref, q_ref, kt_ref, v_ref):
    ua = ua_ref[0]
    kv_lo = MLA_Q_RANK + MLA_KV_RANK
    cq = _rms(ua[:, :MLA_Q_RANK], qn_ref[...])
    q = _bdot(cq.astype(BF16), wq_ref[...])
    q = _rope(q, cq_ref[...], sq1_ref[...], sq2_ref[...], MLA_ROPE // 2)
    q_ref[0] = (q * ((MLA_NOPE + MLA_ROPE) ** -0.5)).astype(BF16)
    ckv = _rms(ua[:, MLA_Q_RANK:kv_lo], kvn_ref[...])
    kv = _bdot(ckv.astype(BF16), wkv_ref[...])
    nk = MLA_HEADS * MLA_NOPE
    v_ref[0] = kv[:, nk:].astype(BF16)
    kr = _rope(ua[:, kv_lo:], ck_ref[...], sk1_ref[...], sk2_ref[...], MLA_ROPE // 2)
    krt = kr.T.astype(BF16)
    for h in range(MLA_HEADS):
        kt_ref[0, h, :MLA_NOPE, :] = kv[:, h * MLA_NOPE:(h + 1) * MLA_NOPE].T.astype(BF16)
        kt_ref[0, h, MLA_NOPE:, :] = krt


def _mla_proj(ua, q_norm, kv_norm, wq, wkv, tq, tk):
    b, s, wa = ua.shape
    tm = min(ROW_TILE, s)
    hq = MLA_HEADS * MLA_HEAD_PAD
    hv = MLA_HEADS * MLA_V
    const = lambda bi, m: (0, 0)
    tab = lambda w: pl.BlockSpec((tm, w), lambda bi, m: (m, 0))
    return pl.pallas_call(
        _mla_proj_kernel,
        grid=(b, s // tm),
        in_specs=[pl.BlockSpec((1, tm, wa), lambda bi, m: (bi, m, 0)),
                  pl.BlockSpec((1, MLA_Q_RANK), const),
                  pl.BlockSpec((1, MLA_KV_RANK), const),
                  pl.BlockSpec(wq.shape, const),
                  pl.BlockSpec(wkv.shape, const),
                  tab(MLA_HEAD_PAD), tab(MLA_HEAD_PAD), tab(MLA_HEAD_PAD),
                  tab(LANES), tab(LANES), tab(LANES)],
        out_specs=[pl.BlockSpec((1, tm, hq), lambda bi, m: (bi, m, 0)),
                   pl.BlockSpec((1, MLA_HEADS, MLA_HEAD_PAD, tm), lambda bi, m: (bi, 0, 0, m)),
                   pl.BlockSpec((1, tm, hv), lambda bi, m: (bi, m, 0))],
        out_shape=[jax.ShapeDtypeStruct((b, s, hq), BF16),
                   jax.ShapeDtypeStruct((b, MLA_HEADS, MLA_HEAD_PAD, s), BF16),
                   jax.ShapeDtypeStruct((b, s, hv), BF16)],
        compiler_params=_params(("arbitrary", "arbitrary")),
        name="mla_proj",
    )(ua, q_norm.reshape(1, -1), kv_norm.reshape(1, -1), wq, wkv, *tq, *tk)


class _Softmax:
    def __init__(self):
        self.m = self.l = self.acc = None

    def update(self, sc, v):
        mc = jnp.max(sc, axis=1, keepdims=True)
        m_new = mc if self.m is None else jnp.maximum(self.m, mc)
        e = jnp.exp(sc - m_new)
        ls = jnp.sum(e, axis=1, keepdims=True)
        pv = _bdot(e.astype(BF16), v)
        if self.m is None:
            self.l, self.acc = ls, pv
        else:
            a = jnp.exp(self.m - m_new)
            self.l, self.acc = a * self.l + ls, a * self.acc + pv
        self.m = m_new

    def result(self):
        return self.acc * (1.0 / self.l)


def _kv_chunks(s):
    kc = min(ATTN_KV_CHUNK, s)
    return [(c * kc, (c + 1) * kc) for c in range(s // kc)]


def _mla_attn_kernel(q_ref, kt_ref, v_ref, o_ref):
    q = q_ref[0]
    sm = _Softmax()
    for lo, hi in _kv_chunks(kt_ref.shape[3]):
        sm.update(_bdot(q, kt_ref[0, 0, :, lo:hi]), v_ref[0, lo:hi, :])
    o_ref[0] = sm.result().astype(BF16)


def _mla_attn(q, kt, v):
    b, s, _ = q.shape
    tq = min(ATTN_Q_TILE, s)
    return pl.pallas_call(
        _mla_attn_kernel,
        grid=(b, MLA_HEADS, s // tq),
        in_specs=[pl.BlockSpec((1, tq, MLA_HEAD_PAD), lambda bi, h, i: (bi, i, h)),
                  pl.BlockSpec((1, 1, MLA_HEAD_PAD, s), lambda bi, h, i: (bi, h, 0, 0)),
                  pl.BlockSpec((1, s, MLA_V), lambda bi, h, i: (bi, 0, h))],
        out_specs=pl.BlockSpec((1, tq, MLA_V), lambda bi, h, i: (bi, i, h)),
        out_shape=jax.ShapeDtypeStruct((b, s, MLA_HEADS * MLA_V), BF16),
        compiler_params=_params(("arbitrary", "arbitrary", "arbitrary")),
        name="mla_attn",
    )(q, kt, v)


def _diff_attn_kernel(q_ref, kt_ref, v_ref, lq1_ref, lk1_ref, lq2_ref, lk2_ref, sub_ref, o_ref, *, lam_init):
    q = q_ref[0]
    lane = lax.broadcasted_iota(jnp.int32, q.shape, 1)
    zero = jnp.zeros_like(q)
    q1 = jnp.where(lane < DIFF_QK, q, zero)
    q2 = jnp.where(lane >= DIFF_QK, q, zero)
    lam = (jnp.exp(jnp.sum(lq1_ref[...] * lk1_ref[...], axis=1, keepdims=True))
           - jnp.exp(jnp.sum(lq2_ref[...] * lk2_ref[...], axis=1, keepdims=True)) + lam_init)
    sm1, sm2 = _Softmax(), _Softmax()
    for lo, hi in _kv_chunks(kt_ref.shape[3]):
        kt = kt_ref[0, 0, :, lo:hi]
        v = v_ref[0, lo:hi, :]
        sm1.update(_bdot(q1, kt), v)
        sm2.update(_bdot(q2, kt), v)
    o = sm1.result() - lam * sm2.result()
    o_ref[0] = (_rms(o, sub_ref[...]) * (1.0 - lam_init)).astype(BF16)


def _diff_attn(uc, kt, lq1, lk1, lq2, lk2, subln, lam_init):
    b, s, _ = uc.shape
    tq = min(ATTN_Q_TILE, s)
    vec = lambda n: pl.BlockSpec((1, n), lambda bi, h, i: (0, 0))
    v_block0 = 2 * DIFF_HEADS
    return pl.pallas_call(
        functools.partial(_diff_attn_kernel, lam_init=lam_init),
        grid=(b, DIFF_HEADS, s // tq),
        in_specs=[pl.BlockSpec((1, tq, DIFF_V), lambda bi, h, i: (bi, i, h)),
                  pl.BlockSpec((1, 1, 2 * DIFF_QK, s), lambda bi, h, i: (bi, h, 0, 0)),
                  pl.BlockSpec((1, s, DIFF_V), lambda bi, h, i: (bi, 0, v_block0 + h)),
                  vec(DIFF_QK), vec(DIFF_QK), vec(DIFF_QK), vec(DIFF_QK), vec(DIFF_V)],
        out_specs=pl.BlockSpec((1, tq, DIFF_V), lambda bi, h, i: (bi, i, h)),
        out_shape=jax.ShapeDtypeStruct((b, s, DIFF_HEADS * DIFF_V), BF16),
        compiler_params=_params(("arbitrary", "arbitrary", "arbitrary")),
        name="diff_attn",
    )(uc, kt, uc, lq1.reshape(1, -1), lk1.reshape(1, -1), lq2.reshape(1, -1), lk2.reshape(1, -1),
      subln.reshape(1, -1))


def _kt_kernel(k_ref, o_ref):
    o_ref[0, 0] = k_ref[0].astype(F32).T.astype(BF16)


def _head_transpose(u, first_block, heads, width):
    b, s, _ = u.shape
    tm = min(ROW_TILE, s)
    return pl.pallas_call(
        _kt_kernel,
        grid=(b, heads, s // tm),
        in_specs=[pl.BlockSpec((1, tm, width), lambda bi, h, m: (bi, m, first_block + h))],
        out_specs=pl.BlockSpec((1, 1, width, tm), lambda bi, h, m: (bi, h, 0, m)),
        out_shape=jax.ShapeDtypeStruct((b, heads, width, s), BF16),
        compiler_params=_params(("arbitrary", "arbitrary", "arbitrary")),
        name="head_transpose",
    )(u)


def _na_plan(rows):
    kh = min(NA_WIN_H, rows)
    key_rows = NA_KEYS // GRID_W
    r0 = lambda r: min(max(r - kh // 2, 0), rows - kh)
    starts, cases, sigs = [], [], []
    for p in range(rows // 2):
        sr = min(r0(2 * p), rows - key_rows)
        sig = (r0(2 * p) - 2 * p, r0(2 * p + 1) - 2 * p - 1, sr - 2 * p)
        if sig not in sigs:
            sigs.append(sig)
        starts.append(sr)
        cases.append(sigs.index(sig))
    return np.array(starts, np.int32), np.array(cases, np.int32), sigs, kh


def _na_bias(rpb, rows):
    _, _, sigs, kh = _na_plan(rows)
    j = np.arange(NA_PAIR)
    m = np.arange(NA_KEYS)
    qrow_rel, qcol = j // GRID_W, j % GRID_W
    krow_rel, kcol = m // GRID_W, m % GRID_W
    c0 = np.clip(qcol - NA_WIN_W // 2, 0, GRID_W - NA_WIN_W)
    col_ok = (kcol[None, :] >= c0[:, None]) & (kcol[None, :] < c0[:, None] + NA_WIN_W)
    edge = GRID_W - NA_WIN_W
    ext = jnp.pad(rpb, ((0, 0), (0, 0), (edge, edge)), mode="edge")
    toe = jnp.stack([ext[:, :, GRID_W - 1 - qc:2 * GRID_W - 1 - qc] for qc in range(GRID_W)], axis=2)
    n_dr = 2 * NA_WIN_H - 1
    key_rows = NA_KEYS // GRID_W
    tabs = []
    for (d0, d1, dsr) in sigs:
        r0_rel = np.where(qrow_rel == 0, d0, d1 + 1)
        krel = dsr + krow_rel
        ok = (krel[None, :] >= r0_rel[:, None]) & (krel[None, :] < r0_rel[:, None] + kh) & col_ok
        blocks = jnp.stack(
            [jnp.stack([toe[:, min(max(dsr + kr - qr + NA_WIN_H - 1, 0), n_dr - 1)] for kr in range(key_rows)],
                       axis=2) for qr in range(2)], axis=1)
        bias = blocks.reshape(rpb.shape[0], NA_PAIR, NA_KEYS)
        tabs.append(jnp.where(ok[None], bias, -jnp.inf))
    return jnp.stack(tabs, axis=1).astype(F32)


def _na_kernel(start_ref, case_ref, q_ref, k_ref, v_ref, bias_ref, o_ref):
    def body(p, carry):
        qs = pl.ds(pl.multiple_of(p * NA_PAIR, NA_PAIR), NA_PAIR)
        ks = pl.ds(pl.multiple_of(start_ref[p] * GRID_W, LANES), NA_KEYS)
        q = q_ref[0, qs, :]
        sc = lax.dot_general(q, k_ref[0, ks, :], (((1,), (1,)), ((), ())), preferred_element_type=F32)
        sc = sc * (NA_DIM ** -0.5) + bias_ref[0, case_ref[p]]
        e = jnp.exp(sc - jnp.max(sc, axis=1, keepdims=True))
        pr = e / jnp.sum(e, axis=1, keepdims=True)
        o_ref[0, qs, :] = _bdot(pr.astype(BF16), v_ref[0, ks, :]).astype(BF16)
        return carry

    lax.fori_loop(0, q_ref.shape[1] // NA_PAIR, body, 0)


def _na_attn(ub, bias):
    b, s, _ = ub.shape
    starts, cases, sigs, _ = _na_plan(s // GRID_W)
    blk = lambda off: pl.BlockSpec((1, s, NA_DIM), lambda bi, h, st, cs: (bi, 0, off + h))
    return pl.pallas_call(
        _na_kernel,
        grid_spec=pltpu.PrefetchScalarGridSpec(
            num_scalar_prefetch=2,
            grid=(b, NA_HEADS),
            in_specs=[blk(0), blk(NA_HEADS), blk(2 * NA_HEADS),
                      pl.BlockSpec((1, len(sigs), NA_PAIR, NA_KEYS), lambda bi, h, st, cs: (h, 0, 0, 0))],
            out_specs=pl.BlockSpec((1, s, NA_DIM), lambda bi, h, st, cs: (bi, 0, h)),
        ),
        out_shape=jax.ShapeDtypeStruct((b, s, NA_HEADS * NA_DIM), BF16),
        compiler_params=_params(("arbitrary", "arbitrary")),
        name="na_attn",
    )(jnp.asarray(starts), jnp.asarray(cases), ub, ub, ub, bias)


def _oproj_kernel(oa_ref, ob_ref, oc_ref, x_ref, ga_ref, wo_ref, g1_ref, b1_ref, scf_ref, shf_ref, rw_ref,
                  x1_ref, h2_ref, lg_ref, *, alpha):
    na, nb = oa_ref.shape[2], ob_ref.shape[2]
    acc = (_bdot(oa_ref[0], wo_ref[:na, :]) + _bdot(ob_ref[0], wo_ref[na:na + nb, :])
           + _bdot(oc_ref[0], wo_ref[na + nb:, :]))
    x1 = _ln(alpha * x_ref[0] + (1.0 + ga_ref[0]) * acc) * g1_ref[...] + b1_ref[...]
    x1_ref[0] = x1
    h2 = _ln(x1) * (1.0 + scf_ref[0]) + shf_ref[0]
    h2_ref[0] = h2
    lg_ref[0] = _dot3(h2, rw_ref[...])


def _oproj(oa, ob, oc, x, ga, wo, g1, b1, scf, shf, rw, alpha):
    b, s, d = x.shape
    tm = min(ROW_TILE // 2, s)
    row = lambda w: pl.BlockSpec((1, tm, w), lambda bi, m: (bi, m, 0))
    per_b = pl.BlockSpec((1, 1, d), lambda bi, m: (bi, 0, 0))
    const = lambda shape: pl.BlockSpec(shape, lambda bi, m: (0, 0))
    return pl.pallas_call(
        functools.partial(_oproj_kernel, alpha=alpha),
        grid=(b, s // tm),
        in_specs=[row(oa.shape[2]), row(ob.shape[2]), row(oc.shape[2]), row(d), per_b,
                  const(wo.shape), const((1, d)), const((1, d)), per_b, per_b, const(rw.shape)],
        out_specs=[row(d), row(d), row(LANES)],
        out_shape=[jax.ShapeDtypeStruct((b, s, d), F32), jax.ShapeDtypeStruct((b, s, d), F32),
                   jax.ShapeDtypeStruct((b, s, LANES), F32)],
        compiler_params=_params(("arbitrary", "arbitrary")),
        name="oproj",
    )(oa, ob, oc, x, ga, wo, g1.reshape(1, d), b1.reshape(1, d), scf, shf, rw)


def _first_max(cur, idx, axis, size):
    m = jnp.max(cur, axis=axis, keepdims=True)
    first = jnp.min(jnp.where(cur == m, idx, size), axis=axis, keepdims=True)
    return idx == first, m


def _router_kernel(lg_ref, rb_ref, e_ref, rank_ref, wt_ref, cnt_ref, carry_scr):
    t = pl.program_id(0)
    tn = lg_ref.shape[0]
    gsz = N_EXPERTS // N_GROUPS
    neg = -jnp.inf

    @pl.when(t == 0)
    def _():
        carry_scr[...] = jnp.zeros_like(carry_scr)

    scores = _sigmoid(lg_ref[...].T[:N_EXPERTS, :])
    biased = scores + rb_ref[...]
    b3 = biased.reshape(N_GROUPS, gsz, tn)
    i3 = lax.broadcasted_iota(jnp.int32, b3.shape, 1)
    pick1, m1 = _first_max(b3, i3, 1, gsz)
    m2 = jnp.max(jnp.where(pick1, neg, b3), axis=1, keepdims=True)
    gs = (m1 + m2).reshape(N_GROUPS, tn)
    gi = lax.broadcasted_iota(jnp.int32, gs.shape, 0)
    gsel = jnp.zeros(gs.shape, F32)
    for _ in range(TOPK_GROUPS):
        pick, _m = _first_max(gs, gi, 0, N_GROUPS)
        gsel = jnp.where(pick, 1.0, gsel)
        gs = jnp.where(pick, neg, gs)
    emask = jnp.broadcast_to(gsel.reshape(N_GROUPS, 1, tn), b3.shape).reshape(N_EXPERTS, tn)
    cur = jnp.where(emask > 0.0, biased, neg)
    ei = lax.broadcasted_iota(jnp.int32, cur.shape, 0)
    self_ = jnp.zeros(cur.shape, F32)
    for _ in range(TOP_K):
        pick, _m = _first_max(cur, ei, 0, N_EXPERTS)
        self_ = jnp.where(pick, 1.0, self_)
        cur = jnp.where(pick, neg, cur)
    sel = self_ > 0.0
    w = jnp.where(sel, scores, 0.0)
    gate = w / jnp.sum(w, axis=0, keepdims=True) * ROUTED_SCALE
    selb = self_.astype(BF16)
    lower = (lax.broadcasted_iota(jnp.int32, (N_EXPERTS, N_EXPERTS), 1)
             < lax.broadcasted_iota(jnp.int32, (N_EXPERTS, N_EXPERTS), 0)).astype(BF16)
    kidx = _bdot(lower, selb)
    upper = (lax.broadcasted_iota(jnp.int32, (tn, tn), 0)
             <= lax.broadcasted_iota(jnp.int32, (tn, tn), 1)).astype(BF16)
    rank = _bdot(selb, upper) + carry_scr[...] - 1.0
    carry_scr[...] = carry_scr[...] + jnp.sum(self_, axis=1, keepdims=True)
    cnt_ref[...] = jnp.broadcast_to(carry_scr[...], cnt_ref.shape)
    eif = ei.astype(F32)
    e_rows, r_rows, w_rows = [], [], []
    for k in range(TOP_K):
        mk = sel & (kidx == float(k))
        e_rows.append(jnp.sum(jnp.where(mk, eif, 0.0), axis=0, keepdims=True))
        r_rows.append(jnp.sum(jnp.where(mk, rank, 0.0), axis=0, keepdims=True))
        w_rows.append(jnp.sum(jnp.where(mk, gate, 0.0), axis=0, keepdims=True))
    e_ref[...] = jnp.concatenate(e_rows, axis=0).astype(jnp.int32)
    rank_ref[...] = jnp.concatenate(r_rows, axis=0).astype(jnp.int32)
    wk = jnp.concatenate(w_rows + [jnp.zeros((LANES - TOP_K, tn), F32)], axis=0)
    wt_ref[...] = wk.T


def _router(logits, router_bias):
    n = logits.shape[0]
    tn = min(ROUTER_TILE, n)
    return pl.pallas_call(
        _router_kernel,
        grid=(n // tn,),
        in_specs=[pl.BlockSpec((tn, LANES), lambda t: (t, 0)),
                  pl.BlockSpec((N_EXPERTS, 1), lambda t: (0, 0))],
        out_specs=[pl.BlockSpec((TOP_K, tn), lambda t: (0, t)),
                   pl.BlockSpec((TOP_K, tn), lambda t: (0, t)),
                   pl.BlockSpec((tn, LANES), lambda t: (t, 0)),
                   pl.BlockSpec((N_EXPERTS, LANES), lambda t: (0, 0))],
        out_shape=[jax.ShapeDtypeStruct((TOP_K, n), jnp.int32),
                   jax.ShapeDtypeStruct((TOP_K, n), jnp.int32),
                   jax.ShapeDtypeStruct((n, LANES), F32),
                   jax.ShapeDtypeStruct((N_EXPERTS, LANES), F32)],
        scratch_shapes=[pltpu.VMEM((N_EXPERTS, 1), F32)],
        compiler_params=_params(("arbitrary",)),
        name="router",
    )(logits, router_bias.reshape(N_EXPERTS, 1))


def _moe_plan(eidx, rank, counts, n):
    tm = MOE_TILE
    n_tiles = TOP_K * n // tm + N_EXPERTS
    ntile = (counts + tm - 1) // tm
    tile_end = jnp.cumsum(ntile)
    tile_start = tile_end - ntile
    n_used = tile_end[-1]
    ti = jnp.minimum(jnp.arange(n_tiles, dtype=jnp.int32), n_used - 1)
    te = jnp.minimum(jnp.sum(tile_end[None, :] <= ti[:, None], axis=1), N_EXPERTS - 1).astype(jnp.int32)
    onehot_te = te[:, None] == jnp.arange(N_EXPERTS, dtype=jnp.int32)[None, :]
    start_te = jnp.sum(jnp.where(onehot_te, tile_start[None, :], 0), axis=1)
    count_te = jnp.sum(jnp.where(onehot_te, counts[None, :], 0), axis=1)
    first = (jnp.arange(n_tiles, dtype=jnp.int32) == start_te).astype(jnp.int32)
    nvalid = jnp.clip(count_te - (ti - start_te) * tm, 0, tm).astype(jnp.int32)
    onehot_e = eidx[:, :, None] == jnp.arange(N_EXPERTS, dtype=jnp.int32)[None, None, :]
    slot = (jnp.sum(jnp.where(onehot_e, tile_start[None, None, :], 0), axis=2) * tm + rank).reshape(-1)
    tok = jnp.broadcast_to(jnp.arange(n, dtype=jnp.int32)[None, :], (TOP_K, n))
    row = tok + n * jnp.arange(TOP_K, dtype=jnp.int32)[:, None]
    dst = jnp.zeros((n_tiles * tm,), jnp.int32).at[slot].set(row.reshape(-1))
    src = dst % n
    return (te, first, nvalid, n_used.reshape(1).astype(jnp.int32),
            src.reshape(n_tiles, 1, tm), dst.reshape(n_tiles, 1, tm))


def _moe_kernel(te_ref, first_ref, nvalid_ref, nused_ref, src_cur, src_nxt, dst_cur, dst_prv, h_hbm, wg_ref,
                wu_ref, wd_ref, y_hbm, xbuf0, xbuf1, ybuf0, ybuf1, wgb, wub, wdb, gsem, ssem):
    i = pl.program_id(0)
    tm = xbuf0.shape[0]
    n_used = nused_ref[0]
    nv = nvalid_ref[i]
    xbufs, ybufs = (xbuf0, xbuf1), (ybuf0, ybuf1)

    def gather(src_ref, p):
        for r in range(tm):
            pltpu.make_async_copy(h_hbm.at[pl.ds(src_ref[0, 0, r], 1)], xbufs[p].at[pl.ds(r, 1)],
                                  gsem.at[p]).start()

    def gather_wait(p):
        pltpu.make_async_copy(h_hbm.at[pl.ds(0, tm)], xbufs[p], gsem.at[p]).wait()

    def scatter_row(dst_ref, p, r):
        pltpu.make_async_copy(ybufs[p].at[pl.ds(r, 1)], y_hbm.at[pl.ds(dst_ref[0, 0, r], 1)],
                              ssem.at[p]).start()

    def scatter_wait(p, rows):
        @pl.when(rows == tm)
        def _():
            pltpu.make_async_copy(ybufs[p], y_hbm.at[pl.ds(0, tm)], ssem.at[p]).wait()

        @pl.when(rows < tm)
        def _():
            def row_wait(r, carry):
                pltpu.make_async_copy(ybufs[p].at[pl.ds(0, 1)], y_hbm.at[pl.ds(0, 1)], ssem.at[p]).wait()
                return carry

            lax.fori_loop(0, rows, row_wait, 0)

    def step(p):
        @pl.when(i >= 2)
        def _():
            scatter_wait(p, nvalid_ref[i - 2])

        gather_wait(p)

        @pl.when(first_ref[i] == 1)
        def _():
            wgb[...] = wg_ref[0, 0].astype(BF16)
            wub[...] = wu_ref[0, 0].astype(BF16)
            wdb[...] = wd_ref[0, 0].astype(BF16)

        def main(scatter_prev):
            gather(src_nxt, 1 - p)
            if scatter_prev:
                for r in range(tm):
                    scatter_row(dst_prv, 1 - p, r)
            x = xbufs[p][...].astype(BF16)
            act = _silu(_bdot(x, wgb[...])) * _bdot(x, wub[...])
            ybufs[p][...] = _bdot(act.astype(BF16), wdb[...])

        prev_full = jnp.logical_and(i >= 1, nvalid_ref[jnp.maximum(i - 1, 0)] == tm)

        @pl.when(prev_full)
        def _():
            main(True)

        @pl.when(jnp.logical_not(prev_full))
        def _():
            main(False)

        last = i == n_used - 1

        @pl.when(jnp.logical_or(nv < tm, last))
        def _():
            def row_body(r, carry):
                scatter_row(dst_cur, p, r)
                return carry

            lax.fori_loop(0, nv, row_body, 0)

        @pl.when(last)
        def _():
            gather_wait(1 - p)
            scatter_wait(p, nv)

            @pl.when(i >= 1)
            def _():
                scatter_wait(1 - p, nvalid_ref[i - 1])

    @pl.when(i < n_used)
    def _():
        @pl.when(i == 0)
        def _():
            gather(src_cur, 0)

        for p in range(2):
            @pl.when(i % 2 == p)
            def _():
                step(p)


def _moe(h2, plan, w_gate, w_up, w_down, layer):
    n, d = h2.shape
    te, first, nvalid, n_used, src, dst = plan
    n_tiles, _, tm = src.shape
    f = w_gate.shape[3]
    idx_blk = lambda fn: pl.BlockSpec((1, 1, tm), fn, memory_space=pltpu.SMEM)
    wspec = lambda shape: pl.BlockSpec((1, 1) + shape, lambda i, te_r, *_: (layer, te_r[i], 0, 0))
    return pl.pallas_call(
        _moe_kernel,
        grid_spec=pltpu.PrefetchScalarGridSpec(
            num_scalar_prefetch=4,
            grid=(n_tiles,),
            in_specs=[idx_blk(lambda i, *_: (i, 0, 0)),
                      idx_blk(lambda i, *_: (jnp.minimum(i + 1, n_tiles - 1), 0, 0)),
                      idx_blk(lambda i, *_: (i, 0, 0)),
                      idx_blk(lambda i, *_: (jnp.maximum(i - 1, 0), 0, 0)),
                      pl.BlockSpec(memory_space=pl.ANY),
                      wspec((d, f)), wspec((d, f)), wspec((f, d))],
            out_specs=pl.BlockSpec(memory_space=pl.ANY),
            scratch_shapes=[pltpu.VMEM((tm, d), F32)] * 4
                           + [pltpu.VMEM((d, f), BF16), pltpu.VMEM((d, f), BF16), pltpu.VMEM((f, d), BF16),
                              pltpu.SemaphoreType.DMA((2,)), pltpu.SemaphoreType.DMA((2,))],
        ),
        out_shape=jax.ShapeDtypeStruct((TOP_K * n, d), F32),
        compiler_params=_params(("arbitrary",)),
        name="moe_experts",
    )(te, first, nvalid, n_used, src, src, dst, dst, h2, w_gate, w_up, w_down)


def _combine_kernel(*refs, alpha):
    y_refs = refs[:TOP_K]
    wt_ref, h_ref, x_ref, gf_ref, sg_ref, su_ref, sd_ref, g2_ref, b2_ref, o_ref = refs[TOP_K:]
    wt = wt_ref[...]
    routed = wt[:, 0:1] * y_refs[0][...]
    for k in range(1, TOP_K):
        routed = routed + wt[:, k:k + 1] * y_refs[k][...]
    hb = h_ref[...].astype(BF16)
    act = _silu(_bdot(hb, sg_ref[...])) * _bdot(hb, su_ref[...])
    y = routed + _bdot(act.astype(BF16), sd_ref[...])
    o_ref[...] = _ln(alpha * x_ref[...] + (1.0 + gf_ref[0]) * y) * g2_ref[...] + b2_ref[...]


def _combine(y_tok, wt, h2, x1, gf, sg, su, sd, g2, b2, alpha, seq):
    n, d = h2.shape
    tm = min(COMBINE_TILE, seq)
    nblk = n // tm
    row = pl.BlockSpec((tm, d), lambda i: (i, 0))
    const = lambda shape: pl.BlockSpec(shape, lambda i: (0, 0))
    y_specs = [pl.BlockSpec((tm, d), (lambda i, k=k: (k * nblk + i, 0))) for k in range(TOP_K)]
    return pl.pallas_call(
        functools.partial(_combine_kernel, alpha=alpha),
        grid=(nblk,),
        in_specs=y_specs + [pl.BlockSpec((tm, LANES), lambda i: (i, 0)), row, row,
                            pl.BlockSpec((1, 1, d), lambda i: (i // (seq // tm), 0, 0)),
                            const(sg.shape), const(su.shape), const(sd.shape), const((1, d)), const((1, d))],
        out_specs=row,
        out_shape=jax.ShapeDtypeStruct((n, d), F32),
        compiler_params=_params(("arbitrary",)),
        name="combine",
    )(*([y_tok] * TOP_K), wt, h2, x1, gf, sg, su, sd, g2.reshape(1, d), b2.reshape(1, d))


def _rope_tables(s, half, width, offset, period=None):
    inv = ROPE_THETA ** (-jnp.arange(half, dtype=F32) / half)
    ang = jnp.arange(s, dtype=jnp.int32).astype(F32)[:, None] * inv[None, :]
    cos, sin = jnp.cos(ang), jnp.sin(ang)
    period = period or width
    c = jnp.ones((s, period), F32).at[:, offset:offset + half].set(cos)
    c = c.at[:, offset + half:offset + 2 * half].set(cos)
    s1 = jnp.zeros((s, period), F32).at[:, offset:offset + half].set(-sin)
    s2 = jnp.zeros((s, period), F32).at[:, offset + half:offset + 2 * half].set(sin)
    reps = width // period
    return tuple(jnp.tile(t, (1, reps)) for t in (c, s1, s2))


def kernel(x, c, w_ada, b_ada, w_in, mla_q_norm, mla_w_uq, mla_kv_norm, mla_w_ukv, na_rpb, diff_lq1, diff_lk1,
           diff_lq2, diff_lk2, diff_subln, w_o, ln1_g, ln1_b, router_w, router_bias, exp_w_gate, exp_w_up,
           exp_w_down, sh_w_gate, sh_w_up, sh_w_down, ln2_g, ln2_b):
    b, s, d = x.shape
    depth = w_ada.shape[0]
    n = b * s
    alpha = (2 * depth) ** 0.25
    ada = _ada(c, w_ada, b_ada)
    tab_diff = _rope_tables(s, DIFF_ROT // 2, LANES, 0, period=DIFF_QK)
    tab_q = _rope_tables(s, MLA_ROPE // 2, MLA_HEAD_PAD, MLA_NOPE)
    tab_k = _rope_tables(s, MLA_ROPE // 2, LANES, 0)
    nq_diff = DIFF_HEADS * 2 * DIFF_QK
    a_pad = LANES - MLA_ROPE

    for i in range(depth):
        lam_init = 0.8 - 0.6 * math.exp(-0.3 * i)
        sh_a, sc_a, g_a, sh_f, sc_f, g_f = [a[:, None, :] for a in jnp.split(ada[i], 6, -1)]
        wi = w_in[i]
        w_a = jnp.pad(wi[:, :IN_A], ((0, 0), (0, a_pad))).astype(BF16)
        w_b = wi[:, IN_A:IN_A + IN_B].astype(BF16)
        w_c = wi[:, IN_A + IN_B:].astype(BF16)
        wq = jnp.pad(mla_w_uq[i].reshape(MLA_Q_RANK, MLA_HEADS, MLA_NOPE + MLA_ROPE),
                     ((0, 0), (0, 0), (0, MLA_HEAD_PAD - MLA_NOPE - MLA_ROPE)))
        wq = wq.reshape(MLA_Q_RANK, MLA_HEADS * MLA_HEAD_PAD).astype(BF16)
        wkv3 = mla_w_ukv[i].reshape(MLA_KV_RANK, MLA_HEADS, MLA_NOPE + MLA_V)
        wkv = jnp.concatenate([wkv3[:, :, :MLA_NOPE].reshape(MLA_KV_RANK, -1),
                               wkv3[:, :, MLA_NOPE:].reshape(MLA_KV_RANK, -1)], axis=1).astype(BF16)
        rw = jnp.pad(router_w[i], ((0, 0), (0, LANES - N_EXPERTS)))

        ua = _inproj(x, sc_a, sh_a, w_a, F32, "inproj_mla")
        ub = _inproj(x, sc_a, sh_a, w_b, BF16, "inproj_na")
        uc = _inproj(x, sc_a, sh_a, w_c, BF16, "inproj_diff", tables=tab_diff, rope_cols=2 * nq_diff,
                     q_cols=nq_diff, q_scale=DIFF_QK ** -0.5, half=DIFF_ROT // 2)
        q_a, kt_a, v_a = _mla_proj(ua, mla_q_norm[i], mla_kv_norm[i], wq, wkv, tab_q, tab_k)
        o_a = _mla_attn(q_a, kt_a, v_a)
        o_b = _na_attn(ub, _na_bias(na_rpb[i], s // GRID_W))
        kt_c = _head_transpose(uc, DIFF_HEADS, DIFF_HEADS, 2 * DIFF_QK)
        o_c = _diff_attn(uc, kt_c, diff_lq1[i], diff_lk1[i], diff_lq2[i], diff_lk2[i], diff_subln[i], lam_init)
        x1, h2, logits = _oproj(o_a, o_b, o_c, x, g_a, w_o[i].astype(BF16), ln1_g[i], ln1_b[i], sc_f, sh_f,
                                rw, alpha)

        h2 = h2.reshape(n, d)
        eidx, rank, wt, cnt = _router(logits.reshape(n, LANES), router_bias[i])
        plan = _moe_plan(eidx, rank, cnt[:, 0].astype(jnp.int32), n)
        y_tok = _moe(h2, plan, exp_w_gate, exp_w_up, exp_w_down, i)
        x = _combine(y_tok, wt, h2, x1.reshape(n, d), g_f, sh_w_gate[i].astype(BF16), sh_w_up[i].astype(BF16),
                     sh_w_down[i].astype(BF16), ln2_g[i], ln2_b[i], alpha, s).reshape(b, s, d)
    return x
```

```python
import functools
import math

import numpy as np
import jax
import jax.numpy as jnp
from jax import lax
from jax.experimental import pallas as pl
from jax.experimental.pallas import tpu as pltpu

F32 = jnp.float32
BF16 = jnp.bfloat16

GRID_W = 64
ROPE_THETA = 500000.0
MLA_HEADS, MLA_Q_RANK, MLA_KV_RANK, MLA_NOPE, MLA_ROPE, MLA_V = 8, 512, 256, 128, 64, 128
NA_HEADS, NA_DIM, NA_WIN_H, NA_WIN_W = 4, 128, 8, 16
DIFF_HEADS, DIFF_QK = 4, 64
DIFF_V = 2 * DIFF_QK
DIFF_ROT = DIFF_QK // 4
IN_A = MLA_Q_RANK + MLA_KV_RANK + MLA_ROPE
IN_B = 3 * NA_HEADS * NA_DIM
N_EXPERTS, TOP_K, N_GROUPS, TOPK_GROUPS = 64, 8, 8, 4
ROUTED_SCALE = 2.5
LN_EPS = 1e-5
RMS_EPS = 1e-6

LANES = 128
MLA_HEAD_PAD = 256
VMEM_LIMIT = 56 * 1024 * 1024
ROW_TILE = 512
ATTN_Q_TILE = 512
ATTN_KV_CHUNK = 512
NA_PAIR = 2 * GRID_W
NA_KEYS = 10 * GRID_W
MOE_TILE = 256
TOKEN_BLOCK = LANES
SLOT_GROUP = 8
COMBINE_TILE = 128
ROUTER_TILE = 512


def _params(sem, vmem=VMEM_LIMIT):
    return pltpu.CompilerParams(dimension_semantics=sem, vmem_limit_bytes=vmem)


def _sigmoid(x):
    return 1.0 / (1.0 + jnp.exp(-x))


def _silu(x):
    return x * _sigmoid(x)


def _ln(x):
    mu = jnp.mean(x, axis=-1, keepdims=True)
    xc = x - mu
    var = jnp.mean(xc * xc, axis=-1, keepdims=True)
    return xc * lax.rsqrt(var + LN_EPS)


def _rms(x, g):
    return x * lax.rsqrt(jnp.mean(x * x, axis=-1, keepdims=True) + RMS_EPS) * g


def _bdot(a, b):
    return jnp.dot(a, b, preferred_element_type=F32)


def _split(a):
    hi = a.astype(BF16)
    lo = (a - hi.astype(F32)).astype(BF16)
    return hi, lo


def _dot3(a, b):
    ah, al = _split(a)
    bh, bl = _split(b)
    return _bdot(ah, bh) + (_bdot(al, bh) + _bdot(ah, bl))


def _rope(x, c, s1, s2, half):
    width = x.shape[1]
    reps = width // c.shape[1]
    if reps > 1:
        c, s1, s2 = (jnp.concatenate([t] * reps, axis=1) for t in (c, s1, s2))
    return x * c + pltpu.roll(x, width - half, 1) * s1 + pltpu.roll(x, half, 1) * s2


def _ada_kernel(c_ref, w_ref, b_ref, o_ref):
    o_ref[0] = _dot3(_silu(c_ref[...]), w_ref[0]) + b_ref[0]


def _ada(c, w_ada, b_ada):
    depth, d, d6 = w_ada.shape
    batch = c.shape[0]
    rows = 8
    cp = jnp.zeros((rows, d), F32).at[:batch].set(c)
    tn = 512
    out = pl.pallas_call(
        _ada_kernel,
        grid=(depth, d6 // tn),
        in_specs=[pl.BlockSpec((rows, d), lambda l, n: (0, 0)),
                  pl.BlockSpec((1, d, tn), lambda l, n: (l, 0, n)),
                  pl.BlockSpec((1, 1, tn), lambda l, n: (l, 0, n))],
        out_specs=pl.BlockSpec((1, rows, tn), lambda l, n: (l, 0, n)),
        out_shape=jax.ShapeDtypeStruct((depth, rows, d6), F32),
        compiler_params=_params(("arbitrary", "arbitrary")),
        name="ada",
    )(cp, w_ada, b_ada.reshape(depth, 1, d6))
    return out[:, :batch]


def _inproj_kernel(x_ref, sc_ref, sh_ref, w_ref, *rest, rope_cols, q_cols, q_scale, half):
    o_ref = rest[-1]
    h = _ln(x_ref[0]) * (1.0 + sc_ref[0]) + sh_ref[0]
    acc = _bdot(h.astype(BF16), w_ref[...])
    if rope_cols:
        c_ref, s1_ref, s2_ref = rest[:3]
        r = _rope(acc[:, :rope_cols], c_ref[...], s1_ref[...], s2_ref[...], half)
        o_ref[0, :, :q_cols] = (r[:, :q_cols] * q_scale).astype(o_ref.dtype)
        o_ref[0, :, q_cols:rope_cols] = r[:, q_cols:].astype(o_ref.dtype)
        o_ref[0, :, rope_cols:] = acc[:, rope_cols:].astype(o_ref.dtype)
    else:
        o_ref[0] = acc.astype(o_ref.dtype)


def _inproj(x, sc, sh, w, out_dtype, name, tables=None, rope_cols=0, q_cols=0, q_scale=1.0, half=0):
    b, s, d = x.shape
    n = w.shape[1]
    tm = min(ROW_TILE, s)
    in_specs = [pl.BlockSpec((1, tm, d), lambda bi, m: (bi, m, 0)),
                pl.BlockSpec((1, 1, d), lambda bi, m: (bi, 0, 0)),
                pl.BlockSpec((1, 1, d), lambda bi, m: (bi, 0, 0)),
                pl.BlockSpec((d, n), lambda bi, m: (0, 0))]
    args = [x, sc, sh, w]
    if tables is not None:
        tw = tables[0].shape[1]
        in_specs += [pl.BlockSpec((tm, tw), lambda bi, m: (m, 0))] * 3
        args += list(tables)
    return pl.pallas_call(
        functools.partial(_inproj_kernel, rope_cols=rope_cols, q_cols=q_cols, q_scale=q_scale, half=half),
        grid=(b, s // tm),
        in_specs=in_specs,
        out_specs=pl.BlockSpec((1, tm, n), lambda bi, m: (bi, m, 0)),
        out_shape=jax.ShapeDtypeStruct((b, s, n), out_dtype),
        compiler_params=_params(("arbitrary", "arbitrary")),
        name=name,
    )(*args)


def _mla_proj_kernel(ua_ref, qn_ref, kvn_ref, wq_ref, wkv_ref, cq_ref, sq1_ref, sq2_ref,
                     ck_ref, sk1_ref, sk2_ref, q_ref, kt_ref, v_ref):
    ua = ua_ref[0]
    kv_lo = MLA_Q_RANK + MLA_KV_RANK
    cq = _rms(ua[:, :MLA_Q_RANK], qn_ref[...])
    q = _bdot(cq.astype(BF16), wq_ref[...])
    q = _rope(q, cq_ref[...], sq1_ref[...], sq2_ref[...], MLA_ROPE // 2)
    q_ref[0] = (q * ((MLA_NOPE + MLA_ROPE) ** -0.5)).astype(BF16)
    ckv = _rms(ua[:, MLA_Q_RANK:kv_lo], kvn_ref[...])
    kv = _bdot(ckv.astype(BF16), wkv_ref[...])
    nk = MLA_HEADS * MLA_NOPE
    v_ref[0] = kv[:, nk:].astype(BF16)
    kr = _rope(ua[:, kv_lo:], ck_ref[...], sk1_ref[...], sk2_ref[...], MLA_ROPE // 2)
    krt = kr.T.astype(BF16)
    for h in range(MLA_HEADS):
        kt_ref[0, h, :MLA_NOPE, :] = kv[:, h * MLA_NOPE:(h + 1) * MLA_NOPE].T.astype(BF16)
        kt_ref[0, h, MLA_NOPE:, :] = krt


def _mla_proj(ua, q_norm, kv_norm, wq, wkv, tq, tk):
    b, s, wa = ua.shape
    tm = min(ROW_TILE, s)
    hq = MLA_HEADS * MLA_HEAD_PAD
    hv = MLA_HEADS * MLA_V
    const = lambda bi, m: (0, 0)
    tab = lambda w: pl.BlockSpec((tm, w), lambda bi, m: (m, 0))
    return pl.pallas_call(
        _mla_proj_kernel,
        grid=(b, s // tm),
        in_specs=[pl.BlockSpec((1, tm, wa), lambda bi, m: (bi, m, 0)),
                  pl.BlockSpec((1, MLA_Q_RANK), const),
                  pl.BlockSpec((1, MLA_KV_RANK), const),
                  pl.BlockSpec(wq.shape, const),
                  pl.BlockSpec(wkv.shape, const),
                  tab(MLA_HEAD_PAD), tab(MLA_HEAD_PAD), tab(MLA_HEAD_PAD),
                  tab(LANES), tab(LANES), tab(LANES)],
        out_specs=[pl.BlockSpec((1, tm, hq), lambda bi, m: (bi, m, 0)),
                   pl.BlockSpec((1, MLA_HEADS, MLA_HEAD_PAD, tm), lambda bi, m: (bi, 0, 0, m)),
                   pl.BlockSpec((1, tm, hv), lambda bi, m: (bi, m, 0))],
        out_shape=[jax.ShapeDtypeStruct((b, s, hq), BF16),
                   jax.ShapeDtypeStruct((b, MLA_HEADS, MLA_HEAD_PAD, s), BF16),
                   jax.ShapeDtypeStruct((b, s, hv), BF16)],
        compiler_params=_params(("arbitrary", "arbitrary")),
        name="mla_proj",
    )(ua, q_norm.reshape(1, -1), kv_norm.reshape(1, -1), wq, wkv, *tq, *tk)


class _Softmax:
    def __init__(self):
        self.m = self.acc = None

    def update(self, sc, v_ones):
        mc = jnp.max(sc, axis=1, keepdims=True)
        m_new = mc if self.m is None else jnp.maximum(self.m, mc)
        pv = _bdot(jnp.exp(sc - m_new).astype(BF16), v_ones)
        self.acc = pv if self.m is None else jnp.exp(self.m - m_new) * self.acc + pv
        self.m = m_new

    def result(self):
        dv = self.acc.shape[1] - LANES
        return self.acc[:, :dv] * (1.0 / self.acc[:, dv:dv + 1])


def _kv_chunks(s):
    kc = min(ATTN_KV_CHUNK, s)
    return [(c * kc, (c + 1) * kc) for c in range(s // kc)]


def _with_ones(v):
    return jnp.concatenate([v, jnp.ones((v.shape[0], LANES), v.dtype)], axis=1)


def _mla_attn_kernel(q_ref, kt_ref, v_ref, o_ref):
    q = q_ref[0]
    sm = _Softmax()
    for lo, hi in _kv_chunks(kt_ref.shape[3]):
        sm.update(_bdot(q, kt_ref[0, 0, :, lo:hi]), _with_ones(v_ref[0, lo:hi, :]))
    o_ref[0] = sm.result().astype(BF16)


def _mla_attn(q, kt, v):
    b, s, _ = q.shape
    tq = min(ATTN_Q_TILE, s)
    return pl.pallas_call(
        _mla_attn_kernel,
        grid=(b, MLA_HEADS, s // tq),
        in_specs=[pl.BlockSpec((1, tq, MLA_HEAD_PAD), lambda bi, h, i: (bi, i, h)),
                  pl.BlockSpec((1, 1, MLA_HEAD_PAD, s), lambda bi, h, i: (bi, h, 0, 0)),
                  pl.BlockSpec((1, s, MLA_V), lambda bi, h, i: (bi, 0, h))],
        out_specs=pl.BlockSpec((1, tq, MLA_V), lambda bi, h, i: (bi, i, h)),
        out_shape=jax.ShapeDtypeStruct((b, s, MLA_HEADS * MLA_V), BF16),
        compiler_params=_params(("arbitrary", "arbitrary", "arbitrary")),
        name="mla_attn",
    )(q, kt, v)


def _diff_attn_kernel(q_ref, kt_ref, v_ref, lq1_ref, lk1_ref, lq2_ref, lk2_ref, sub_ref, o_ref, *, lam_init):
    q = q_ref[0]
    lane = lax.broadcasted_iota(jnp.int32, q.shape, 1)
    zero = jnp.zeros_like(q)
    q1 = jnp.where(lane < DIFF_QK, q, zero)
    q2 = jnp.where(lane >= DIFF_QK, q, zero)
    lam = (jnp.exp(jnp.sum(lq1_ref[...] * lk1_ref[...], axis=1, keepdims=True))
           - jnp.exp(jnp.sum(lq2_ref[...] * lk2_ref[...], axis=1, keepdims=True)) + lam_init)
    sm1, sm2 = _Softmax(), _Softmax()
    for lo, hi in _kv_chunks(kt_ref.shape[3]):
        kt = kt_ref[0, 0, :, lo:hi]
        v = _with_ones(v_ref[0, lo:hi, :])
        sm1.update(_bdot(q1, kt), v)
        sm2.update(_bdot(q2, kt), v)
    o = sm1.result() - lam * sm2.result()
    o_ref[0] = (_rms(o, sub_ref[...]) * (1.0 - lam_init)).astype(BF16)


def _diff_attn(uc, kt, lq1, lk1, lq2, lk2, subln, lam_init):
    b, s, _ = uc.shape
    tq = min(ATTN_Q_TILE, s)
    vec = lambda n: pl.BlockSpec((1, n), lambda bi, h, i: (0, 0))
    v_block0 = 2 * DIFF_HEADS
    return pl.pallas_call(
        functools.partial(_diff_attn_kernel, lam_init=lam_init),
        grid=(b, DIFF_HEADS, s // tq),
        in_specs=[pl.BlockSpec((1, tq, DIFF_V), lambda bi, h, i: (bi, i, h)),
                  pl.BlockSpec((1, 1, 2 * DIFF_QK, s), lambda bi, h, i: (bi, h, 0, 0)),
                  pl.BlockSpec((1, s, DIFF_V), lambda bi, h, i: (bi, 0, v_block0 + h)),
                  vec(DIFF_QK), vec(DIFF_QK), vec(DIFF_QK), vec(DIFF_QK), vec(DIFF_V)],
        out_specs=pl.BlockSpec((1, tq, DIFF_V), lambda bi, h, i: (bi, i, h)),
        out_shape=jax.ShapeDtypeStruct((b, s, DIFF_HEADS * DIFF_V), BF16),
        compiler_params=_params(("arbitrary", "arbitrary", "arbitrary")),
        name="diff_attn",
    )(uc, kt, uc, lq1.reshape(1, -1), lk1.reshape(1, -1), lq2.reshape(1, -1), lk2.reshape(1, -1),
      subln.reshape(1, -1))


def _kt_kernel(k_ref, o_ref):
    o_ref[0, 0] = k_ref[0].astype(F32).T.astype(BF16)


def _head_transpose(u, first_block, heads, width):
    b, s, _ = u.shape
    tm = min(ROW_TILE, s)
    return pl.pallas_call(
        _kt_kernel,
        grid=(b, heads, s // tm),
        in_specs=[pl.BlockSpec((1, tm, width), lambda bi, h, m: (bi, m, first_block + h))],
        out_specs=pl.BlockSpec((1, 1, width, tm), lambda bi, h, m: (bi, h, 0, m)),
        out_shape=jax.ShapeDtypeStruct((b, heads, width, s), BF16),
        compiler_params=_params(("arbitrary", "arbitrary", "arbitrary")),
        name="head_transpose",
    )(u)


def _na_plan(rows):
    kh = min(NA_WIN_H, rows)
    key_rows = NA_KEYS // GRID_W
    r0 = lambda r: min(max(r - kh // 2, 0), rows - kh)
    starts, cases, sigs = [], [], []
    for p in range(rows // 2):
        sr = min(r0(2 * p), rows - key_rows)
        sig = (r0(2 * p) - 2 * p, r0(2 * p + 1) - 2 * p - 1, sr - 2 * p)
        if sig not in sigs:
            sigs.append(sig)
        starts.append(sr)
        cases.append(sigs.index(sig))
    return np.array(starts, np.int32), np.array(cases, np.int32), sigs, kh


def _na_bias(rpb, rows):
    _, _, sigs, kh = _na_plan(rows)
    j = np.arange(NA_PAIR)
    m = np.arange(NA_KEYS)
    qrow_rel, qcol = j // GRID_W, j % GRID_W
    krow_rel, kcol = m // GRID_W, m % GRID_W
    c0 = np.clip(qcol - NA_WIN_W // 2, 0, GRID_W - NA_WIN_W)
    col_ok = (kcol[None, :] >= c0[:, None]) & (kcol[None, :] < c0[:, None] + NA_WIN_W)
    edge = GRID_W - NA_WIN_W
    ext = jnp.pad(rpb, ((0, 0), (0, 0), (edge, edge)), mode="edge")
    toe = jnp.stack([ext[:, :, GRID_W - 1 - qc:2 * GRID_W - 1 - qc] for qc in range(GRID_W)], axis=2)
    n_dr = 2 * NA_WIN_H - 1
    key_rows = NA_KEYS // GRID_W
    tabs = []
    for (d0, d1, dsr) in sigs:
        r0_rel = np.where(qrow_rel == 0, d0, d1 + 1)
        krel = dsr + krow_rel
        ok = (krel[None, :] >= r0_rel[:, None]) & (krel[None, :] < r0_rel[:, None] + kh) & col_ok
        blocks = jnp.stack(
            [jnp.stack([toe[:, min(max(dsr + kr - qr + NA_WIN_H - 1, 0), n_dr - 1)] for kr in range(key_rows)],
                       axis=2) for qr in range(2)], axis=1)
        bias = blocks.reshape(rpb.shape[0], NA_PAIR, NA_KEYS)
        tabs.append(jnp.where(ok[None], bias, -jnp.inf))
    return jnp.stack(tabs, axis=1).astype(F32)


def _na_kernel(start_ref, case_ref, q_ref, k_ref, v_ref, bias_ref, o_ref):
    def body(p, carry):
        qs = pl.ds(pl.multiple_of(p * NA_PAIR, NA_PAIR), NA_PAIR)
        ks = pl.ds(pl.multiple_of(start_ref[p] * GRID_W, LANES), NA_KEYS)
        q = q_ref[0, qs, :]
        sc = lax.dot_general(q, k_ref[0, ks, :], (((1,), (1,)), ((), ())), preferred_element_type=F32)
        sc = sc * (NA_DIM ** -0.5) + bias_ref[0, case_ref[p]]
        e = jnp.exp(sc - jnp.max(sc, axis=1, keepdims=True))
        pr = e / jnp.sum(e, axis=1, keepdims=True)
        o_ref[0, qs, :] = _bdot(pr.astype(BF16), v_ref[0, ks, :]).astype(BF16)
        return carry

    lax.fori_loop(0, q_ref.shape[1] // NA_PAIR, body, 0, unroll=4)


def _na_attn(ub, bias):
    b, s, _ = ub.shape
    starts, cases, sigs, _ = _na_plan(s // GRID_W)
    blk = lambda off: pl.BlockSpec((1, s, NA_DIM), lambda bi, h, st, cs: (bi, 0, off + h))
    return pl.pallas_call(
        _na_kernel,
        grid_spec=pltpu.PrefetchScalarGridSpec(
            num_scalar_prefetch=2,
            grid=(b, NA_HEADS),
            in_specs=[blk(0), blk(NA_HEADS), blk(2 * NA_HEADS),
                      pl.BlockSpec((1, len(sigs), NA_PAIR, NA_KEYS), lambda bi, h, st, cs: (h, 0, 0, 0))],
            out_specs=pl.BlockSpec((1, s, NA_DIM), lambda bi, h, st, cs: (bi, 0, h)),
        ),
        out_shape=jax.ShapeDtypeStruct((b, s, NA_HEADS * NA_DIM), BF16),
        compiler_params=_params(("arbitrary", "arbitrary")),
        name="na_attn",
    )(jnp.asarray(starts), jnp.asarray(cases), ub, ub, ub, bias)


def _oproj_kernel(oa_ref, ob_ref, oc_ref, x_ref, ga_ref, wo_ref, g1_ref, b1_ref, scf_ref, shf_ref, rw_ref,
                  x1_ref, h2_ref, lg_ref, *, alpha):
    na, nb = oa_ref.shape[2], ob_ref.shape[2]
    acc = (_bdot(oa_ref[0], wo_ref[:na, :]) + _bdot(ob_ref[0], wo_ref[na:na + nb, :])
           + _bdot(oc_ref[0], wo_ref[na + nb:, :]))
    x1 = _ln(alpha * x_ref[0] + (1.0 + ga_ref[0]) * acc) * g1_ref[...] + b1_ref[...]
    x1_ref[0] = x1
    h2 = _ln(x1) * (1.0 + scf_ref[0]) + shf_ref[0]
    h2_ref[0] = h2
    lg_ref[0] = _dot3(h2, rw_ref[...])


def _oproj(oa, ob, oc, x, ga, wo, g1, b1, scf, shf, rw, alpha):
    b, s, d = x.shape
    tm = min(ROW_TILE // 2, s)
    row = lambda w: pl.BlockSpec((1, tm, w), lambda bi, m: (bi, m, 0))
    per_b = pl.BlockSpec((1, 1, d), lambda bi, m: (bi, 0, 0))
    const = lambda shape: pl.BlockSpec(shape, lambda bi, m: (0, 0))
    return pl.pallas_call(
        functools.partial(_oproj_kernel, alpha=alpha),
        grid=(b, s // tm),
        in_specs=[row(oa.shape[2]), row(ob.shape[2]), row(oc.shape[2]), row(d), per_b,
                  const(wo.shape), const((1, d)), const((1, d)), per_b, per_b, const(rw.shape)],
        out_specs=[row(d), row(d), row(LANES)],
        out_shape=[jax.ShapeDtypeStruct((b, s, d), F32), jax.ShapeDtypeStruct((b, s, d), F32),
                   jax.ShapeDtypeStruct((b, s, LANES), F32)],
        compiler_params=_params(("arbitrary", "arbitrary")),
        name="oproj",
    )(oa, ob, oc, x, ga, wo, g1.reshape(1, d), b1.reshape(1, d), scf, shf, rw)


def _first_max(cur, idx, axis, size):
    m = jnp.max(cur, axis=axis, keepdims=True)
    first = jnp.min(jnp.where(cur == m, idx, size), axis=axis, keepdims=True)
    return idx == first, m


def _router_kernel(lg_ref, rb_ref, wt_ref, lc_ref, ki_ref):
    tn = lg_ref.shape[0]
    gsz = N_EXPERTS // N_GROUPS
    neg = -jnp.inf
    scores = _sigmoid(lg_ref[...].T[:N_EXPERTS, :])
    biased = scores + rb_ref[...]
    b3 = biased.reshape(N_GROUPS, gsz, tn)
    i3 = lax.broadcasted_iota(jnp.int32, b3.shape, 1)
    pick1, m1 = _first_max(b3, i3, 1, gsz)
    m2 = jnp.max(jnp.where(pick1, neg, b3), axis=1, keepdims=True)
    gs = (m1 + m2).reshape(N_GROUPS, tn)
    gi = lax.broadcasted_iota(jnp.int32, gs.shape, 0)
    gsel = jnp.zeros(gs.shape, F32)
    for _ in range(TOPK_GROUPS):
        pick, _m = _first_max(gs, gi, 0, N_GROUPS)
        gsel = jnp.where(pick, 1.0, gsel)
        gs = jnp.where(pick, neg, gs)
    emask = jnp.broadcast_to(gsel.reshape(N_GROUPS, 1, tn), b3.shape).reshape(N_EXPERTS, tn)
    cur = jnp.where(emask > 0.0, biased, neg)
    ei = lax.broadcasted_iota(jnp.int32, cur.shape, 0)
    self_ = jnp.zeros(cur.shape, F32)
    for _ in range(TOP_K):
        pick, _m = _first_max(cur, ei, 0, N_EXPERTS)
        self_ = jnp.where(pick, 1.0, self_)
        cur = jnp.where(pick, neg, cur)
    sel = self_ > 0.0
    w = jnp.where(sel, scores, 0.0)
    gate = w / jnp.sum(w, axis=0, keepdims=True) * ROUTED_SCALE
    selb = self_.astype(BF16)
    lower = (lax.broadcasted_iota(jnp.int32, (N_EXPERTS, N_EXPERTS), 1)
             < lax.broadcasted_iota(jnp.int32, (N_EXPERTS, N_EXPERTS), 0)).astype(BF16)
    kidx = _bdot(lower, selb)
    ki_ref[...] = kidx.astype(BF16)
    ti = lax.broadcasted_iota(jnp.int32, (tn, tn), 0)
    tj = lax.broadcasted_iota(jnp.int32, (tn, tn), 1)
    same_block = (ti // TOKEN_BLOCK) == (tj // TOKEN_BLOCK)
    lc_ref[...] = _bdot(selb, jnp.logical_and(ti <= tj, same_block).astype(BF16)).astype(BF16)
    w_rows = [jnp.sum(jnp.where(sel & (kidx == float(k)), gate, 0.0), axis=0, keepdims=True)
              for k in range(TOP_K)]
    wk = jnp.concatenate(w_rows + [jnp.zeros((LANES - TOP_K, tn), F32)], axis=0)
    wt_ref[...] = wk.T


def _router(logits, router_bias):
    n = logits.shape[0]
    tn = min(ROUTER_TILE, n)
    return pl.pallas_call(
        _router_kernel,
        grid=(n // tn,),
        in_specs=[pl.BlockSpec((tn, LANES), lambda t: (t, 0)),
                  pl.BlockSpec((N_EXPERTS, 1), lambda t: (0, 0))],
        out_specs=[pl.BlockSpec((tn, LANES), lambda t: (t, 0)),
                   pl.BlockSpec((N_EXPERTS, tn), lambda t: (0, t)),
                   pl.BlockSpec((N_EXPERTS, tn), lambda t: (0, t))],
        out_shape=[jax.ShapeDtypeStruct((n, LANES), F32),
                   jax.ShapeDtypeStruct((N_EXPERTS, n), BF16),
                   jax.ShapeDtypeStruct((N_EXPERTS, n), BF16)],
        compiler_params=_params(("arbitrary",)),
        name="router",
    )(logits, router_bias.reshape(N_EXPERTS, 1))


def _slots_kernel(te_ref, r0_ref, lc_ref, ki_ref, pin_ref, pex_ref, src_ref, dst_ref, *, n):
    g = pl.program_id(0)
    tm = src_ref.shape[2]
    nb = lc_ref.shape[1]
    slot_i = lax.broadcasted_iota(jnp.int32, (tm, 1), 0).astype(F32)
    blk_i = lax.broadcasted_iota(jnp.int32, (tm, nb), 1).astype(F32)
    lane_i = lax.broadcasted_iota(jnp.int32, (tm, TOKEN_BLOCK), 1).astype(F32)

    def lanes(col):
        return jnp.broadcast_to(col, (tm, LANES)).T[0:1, :].astype(jnp.int32)

    for j in range(SLOT_GROUP):
        i = g * SLOT_GROUP + j
        e = te_ref[i]
        r = slot_i + r0_ref[i].astype(F32)
        pin = pin_ref[e]
        blk = jnp.sum((pin <= r).astype(F32), axis=1, keepdims=True)
        onehot = (blk_i == blk).astype(F32)
        r_loc = r - jnp.sum(onehot * pex_ref[e], axis=1, keepdims=True)
        ohb = onehot.astype(BF16)
        lc_rows = _bdot(ohb, lc_ref[e])
        tl = jnp.sum((lc_rows <= r_loc).astype(F32), axis=1, keepdims=True)
        kk = jnp.sum(jnp.where(lane_i == tl, _bdot(ohb, ki_ref[e]), 0.0), axis=1, keepdims=True)
        tok = blk * TOKEN_BLOCK + tl
        valid = blk < nb
        src_ref[j] = lanes(jnp.where(valid, tok, 0.0))
        dst_ref[j] = lanes(jnp.where(valid, kk * n + tok, 0.0))


def _moe_plan(lc, ki, n):
    tm = MOE_TILE
    n_tiles = TOP_K * n // tm + N_EXPERTS
    nb = n // TOKEN_BLOCK
    lc3 = lc.reshape(N_EXPERTS, nb, TOKEN_BLOCK)
    cb = lc3[:, :, TOKEN_BLOCK - 1].astype(F32)
    pin = jnp.cumsum(cb, axis=1)
    pex = pin - cb
    counts = pin[:, nb - 1].astype(jnp.int32)
    ntile = (counts + tm - 1) // tm
    tile_end = jnp.cumsum(ntile)
    tile_start = tile_end - ntile
    n_used = tile_end[-1]
    ti = jnp.minimum(jnp.arange(n_tiles, dtype=jnp.int32), n_used - 1)
    te = jnp.minimum(jnp.sum(tile_end[None, :] <= ti[:, None], axis=1), N_EXPERTS - 1).astype(jnp.int32)
    onehot_te = te[:, None] == jnp.arange(N_EXPERTS, dtype=jnp.int32)[None, :]
    start_te = jnp.sum(jnp.where(onehot_te, tile_start[None, :], 0), axis=1)
    count_te = jnp.sum(jnp.where(onehot_te, counts[None, :], 0), axis=1)
    first = (jnp.arange(n_tiles, dtype=jnp.int32) == start_te).astype(jnp.int32)
    r0 = ((ti - start_te) * tm).astype(jnp.int32)
    nvalid = jnp.clip(count_te - r0, 0, tm).astype(jnp.int32)
    ar = jnp.arange(N_EXPERTS, dtype=jnp.int32)
    later_used = jnp.logical_and(ar[None, :] > ar[:, None], (counts > 0)[None, :])
    next_used = jnp.min(jnp.where(later_used, ar[None, :], N_EXPERTS), axis=1)
    nxt_e = jnp.sum(jnp.where(onehot_te, next_used[None, :], 0), axis=1).astype(jnp.int32)
    whole = lambda shape: pl.BlockSpec(shape, lambda g, *_: (0,) * len(shape))
    out_blk = pl.BlockSpec((SLOT_GROUP, 1, tm), lambda g, *_: (g, 0, 0))
    src, dst = pl.pallas_call(
        functools.partial(_slots_kernel, n=n),
        grid_spec=pltpu.PrefetchScalarGridSpec(
            num_scalar_prefetch=2,
            grid=(n_tiles // SLOT_GROUP,),
            in_specs=[whole((N_EXPERTS, nb, TOKEN_BLOCK)), whole((N_EXPERTS, nb, TOKEN_BLOCK)),
                      whole((N_EXPERTS, 1, nb)), whole((N_EXPERTS, 1, nb))],
            out_specs=[out_blk, out_blk],
        ),
        out_shape=[jax.ShapeDtypeStruct((n_tiles, 1, tm), jnp.int32)] * 2,
        compiler_params=_params(("arbitrary",)),
        name="moe_slots",
    )(te, r0, lc3, ki.reshape(N_EXPERTS, nb, TOKEN_BLOCK), pin.reshape(N_EXPERTS, 1, nb),
      pex.reshape(N_EXPERTS, 1, nb))
    return te, first, nvalid, nxt_e, n_used.reshape(1).astype(jnp.int32), src, dst


def _moe_kernel(te_ref, first_ref, nvalid_ref, nxte_ref, nused_ref, src_cur, src_nxt, dst_cur, dst_prv, h_hbm,
                wg_hbm, wu_hbm, wd_hbm, y_hbm, xbuf0, xbuf1, ybuf0, ybuf1, wgf, wuf, wdf, wgb, wub, wdb,
                gsem, ssem, wsem, *, layer):
    i = pl.program_id(0)
    tm = xbuf0.shape[0]
    n_used = nused_ref[0]
    nv = nvalid_ref[i]
    xbufs, ybufs = (xbuf0, xbuf1), (ybuf0, ybuf1)

    def gather(src_ref, p):
        for r in range(tm):
            pltpu.make_async_copy(h_hbm.at[pl.ds(src_ref[0, 0, r], 1)], xbufs[p].at[pl.ds(r, 1)],
                                  gsem.at[p]).start()

    def gather_wait(p):
        pltpu.make_async_copy(h_hbm.at[pl.ds(0, tm)], xbufs[p], gsem.at[p]).wait()

    def scatter_row(dst_ref, p, r):
        pltpu.make_async_copy(ybufs[p].at[pl.ds(r, 1)], y_hbm.at[pl.ds(dst_ref[0, 0, r], 1)],
                              ssem.at[p]).start()

    def scatter_wait(p, rows):
        @pl.when(rows == tm)
        def _():
            pltpu.make_async_copy(ybufs[p], y_hbm.at[pl.ds(0, tm)], ssem.at[p]).wait()

        @pl.when(rows < tm)
        def _():
            def row_wait(r, carry):
                pltpu.make_async_copy(ybufs[p].at[pl.ds(0, 1)], y_hbm.at[pl.ds(0, 1)], ssem.at[p]).wait()
                return carry

            lax.fori_loop(0, rows, row_wait, 0)

    def weight_copies(e):
        return [pltpu.make_async_copy(hbm.at[layer, e], buf, wsem.at[k])
                for k, (hbm, buf) in enumerate(((wg_hbm, wgf), (wu_hbm, wuf), (wd_hbm, wdf)))]

    def step(p):
        @pl.when(i >= 2)
        def _():
            scatter_wait(p, nvalid_ref[i - 2])

        gather_wait(p)

        @pl.when(first_ref[i] == 1)
        def _():
            @pl.when(i == 0)
            def _():
                for cp in weight_copies(te_ref[0]):
                    cp.start(priority=1)

            for cp in weight_copies(te_ref[i]):
                cp.wait()
            wgb[...] = wgf[...].astype(BF16)
            wub[...] = wuf[...].astype(BF16)
            wdb[...] = wdf[...].astype(BF16)
            nxt = nxte_ref[i]

            @pl.when(nxt < N_EXPERTS)
            def _():
                for cp in weight_copies(nxt):
                    cp.start(priority=1)

        def main(scatter_prev):
            gather(src_nxt, 1 - p)
            if scatter_prev:
                for r in range(tm):
                    scatter_row(dst_prv, 1 - p, r)
            x = xbufs[p][...].astype(BF16)
            act = _silu(_bdot(x, wgb[...])) * _bdot(x, wub[...])
            ybufs[p][...] = _bdot(act.astype(BF16), wdb[...])

        prev_full = jnp.logical_and(i >= 1, nvalid_ref[jnp.maximum(i - 1, 0)] == tm)

        @pl.when(prev_full)
        def _():
            main(True)

        @pl.when(jnp.logical_not(prev_full))
        def _():
            main(False)

        last = i == n_used - 1

        @pl.when(jnp.logical_or(nv < tm, last))
        def _():
            def row_body(r, carry):
                scatter_row(dst_cur, p, r)
                return carry

            lax.fori_loop(0, nv, row_body, 0)

        @pl.when(last)
        def _():
            gather_wait(1 - p)
            scatter_wait(p, nv)

            @pl.when(i >= 1)
            def _():
                scatter_wait(1 - p, nvalid_ref[i - 1])

    @pl.when(i < n_used)
    def _():
        @pl.when(i == 0)
        def _():
            gather(src_cur, 0)

        for p in range(2):
            @pl.when(i % 2 == p)
            def _():
                step(p)


def _moe(h2, plan, w_gate, w_up, w_down, layer):
    n, d = h2.shape
    te, first, nvalid, nxt_e, n_used, src, dst = plan
    n_tiles, _, tm = src.shape
    f = w_gate.shape[3]
    idx_blk = lambda fn: pl.BlockSpec((1, 1, tm), fn, memory_space=pltpu.SMEM)
    hbm = pl.BlockSpec(memory_space=pl.ANY)
    return pl.pallas_call(
        functools.partial(_moe_kernel, layer=layer),
        grid_spec=pltpu.PrefetchScalarGridSpec(
            num_scalar_prefetch=5,
            grid=(n_tiles,),
            in_specs=[idx_blk(lambda i, *_: (i, 0, 0)),
                      idx_blk(lambda i, *_: (jnp.minimum(i + 1, n_tiles - 1), 0, 0)),
                      idx_blk(lambda i, *_: (i, 0, 0)),
                      idx_blk(lambda i, *_: (jnp.maximum(i - 1, 0), 0, 0)),
                      hbm, hbm, hbm, hbm],
            out_specs=hbm,
            scratch_shapes=[pltpu.VMEM((tm, d), F32)] * 4
                           + [pltpu.VMEM((d, f), F32), pltpu.VMEM((d, f), F32), pltpu.VMEM((f, d), F32),
                              pltpu.VMEM((d, f), BF16), pltpu.VMEM((d, f), BF16), pltpu.VMEM((f, d), BF16),
                              pltpu.SemaphoreType.DMA((2,)), pltpu.SemaphoreType.DMA((2,)),
                              pltpu.SemaphoreType.DMA((3,))],
        ),
        out_shape=jax.ShapeDtypeStruct((TOP_K * n, d), F32),
        compiler_params=_params(("arbitrary",)),
        name="moe_experts",
    )(te, first, nvalid, nxt_e, n_used, src, src, dst, dst, h2, w_gate, w_up, w_down)


def _combine_kernel(*refs, alpha):
    y_refs = refs[:TOP_K]
    wt_ref, h_ref, x_ref, gf_ref, sg_ref, su_ref, sd_ref, g2_ref, b2_ref, o_ref = refs[TOP_K:]
    wt = wt_ref[...]
    routed = wt[:, 0:1] * y_refs[0][...]
    for k in range(1, TOP_K):
        routed = routed + wt[:, k:k + 1] * y_refs[k][...]
    hb = h_ref[...].astype(BF16)
    act = _silu(_bdot(hb, sg_ref[...])) * _bdot(hb, su_ref[...])
    y = routed + _bdot(act.astype(BF16), sd_ref[...])
    o_ref[...] = _ln(alpha * x_ref[...] + (1.0 + gf_ref[0]) * y) * g2_ref[...] + b2_ref[...]


def _combine(y_tok, wt, h2, x1, gf, sg, su, sd, g2, b2, alpha, seq):
    n, d = h2.shape
    tm = min(COMBINE_TILE, seq)
    nblk = n // tm
    row = pl.BlockSpec((tm, d), lambda i: (i, 0))
    const = lambda shape: pl.BlockSpec(shape, lambda i: (0, 0))
    y_specs = [pl.BlockSpec((tm, d), (lambda i, k=k: (k * nblk + i, 0))) for k in range(TOP_K)]
    return pl.pallas_call(
        functools.partial(_combine_kernel, alpha=alpha),
        grid=(nblk,),
        in_specs=y_specs + [pl.BlockSpec((tm, LANES), lambda i: (i, 0)), row, row,
                            pl.BlockSpec((1, 1, d), lambda i: (i // (seq // tm), 0, 0)),
                            const(sg.shape), const(su.shape), const(sd.shape), const((1, d)), const((1, d))],
        out_specs=row,
        out_shape=jax.ShapeDtypeStruct((n, d), F32),
        compiler_params=_params(("arbitrary",)),
        name="combine",
    )(*([y_tok] * TOP_K), wt, h2, x1, gf, sg, su, sd, g2.reshape(1, d), b2.reshape(1, d))


def _rope_tables(s, half, width, offset, period=None):
    inv = ROPE_THETA ** (-jnp.arange(half, dtype=F32) / half)
    ang = jnp.arange(s, dtype=jnp.int32).astype(F32)[:, None] * inv[None, :]
    cos, sin = jnp.cos(ang), jnp.sin(ang)
    period = period or width
    c = jnp.ones((s, period), F32).at[:, offset:offset + half].set(cos)
    c = c.at[:, offset + half:offset + 2 * half].set(cos)
    s1 = jnp.zeros((s, period), F32).at[:, offset:offset + half].set(-sin)
    s2 = jnp.zeros((s, period), F32).at[:, offset + half:offset + 2 * half].set(sin)
    reps = width // period
    return tuple(jnp.tile(t, (1, reps)) for t in (c, s1, s2))


def kernel(x, c, w_ada, b_ada, w_in, mla_q_norm, mla_w_uq, mla_kv_norm, mla_w_ukv, na_rpb, diff_lq1, diff_lk1,
           diff_lq2, diff_lk2, diff_subln, w_o, ln1_g, ln1_b, router_w, router_bias, exp_w_gate, exp_w_up,
           exp_w_down, sh_w_gate, sh_w_up, sh_w_down, ln2_g, ln2_b):
    b, s, d = x.shape
    depth = w_ada.shape[0]
    n = b * s
    alpha = (2 * depth) ** 0.25
    ada = _ada(c, w_ada, b_ada)
    tab_diff = _rope_tables(s, DIFF_ROT // 2, LANES, 0, period=DIFF_QK)
    tab_q = _rope_tables(s, MLA_ROPE // 2, MLA_HEAD_PAD, MLA_NOPE)
    tab_k = _rope_tables(s, MLA_ROPE // 2, LANES, 0)
    nq_diff = DIFF_HEADS * 2 * DIFF_QK
    a_pad = LANES - MLA_ROPE

    for i in range(depth):
        lam_init = 0.8 - 0.6 * math.exp(-0.3 * i)
        sh_a, sc_a, g_a, sh_f, sc_f, g_f = [a[:, None, :] for a in jnp.split(ada[i], 6, -1)]
        wi = w_in[i]
        w_a = jnp.pad(wi[:, :IN_A], ((0, 0), (0, a_pad))).astype(BF16)
        w_b = wi[:, IN_A:IN_A + IN_B].astype(BF16)
        w_c = wi[:, IN_A + IN_B:].astype(BF16)
        wq = jnp.pad(mla_w_uq[i].reshape(MLA_Q_RANK, MLA_HEADS, MLA_NOPE + MLA_ROPE),
                     ((0, 0), (0, 0), (0, MLA_HEAD_PAD - MLA_NOPE - MLA_ROPE)))
        wq = wq.reshape(MLA_Q_RANK, MLA_HEADS * MLA_HEAD_PAD).astype(BF16)
        wkv3 = mla_w_ukv[i].reshape(MLA_KV_RANK, MLA_HEADS, MLA_NOPE + MLA_V)
        wkv = jnp.concatenate([wkv3[:, :, :MLA_NOPE].reshape(MLA_KV_RANK, -1),
                               wkv3[:, :, MLA_NOPE:].reshape(MLA_KV_RANK, -1)], axis=1).astype(BF16)
        rw = jnp.pad(router_w[i], ((0, 0), (0, LANES - N_EXPERTS)))

        ua = _inproj(x, sc_a, sh_a, w_a, F32, "inproj_mla")
        ub = _inproj(x, sc_a, sh_a, w_b, BF16, "inproj_na")
        uc = _inproj(x, sc_a, sh_a, w_c, BF16, "inproj_diff", tables=tab_diff, rope_cols=2 * nq_diff,
                     q_cols=nq_diff, q_scale=DIFF_QK ** -0.5, half=DIFF_ROT // 2)
        q_a, kt_a, v_a = _mla_proj(ua, mla_q_norm[i], mla_kv_norm[i], wq, wkv, tab_q, tab_k)
        o_a = _mla_attn(q_a, kt_a, v_a)
        o_b = _na_attn(ub, _na_bias(na_rpb[i], s // GRID_W))
        kt_c = _head_transpose(uc, DIFF_HEADS, DIFF_HEADS, 2 * DIFF_QK)
        o_c = _diff_attn(uc, kt_c, diff_lq1[i], diff_lk1[i], diff_lq2[i], diff_lk2[i], diff_subln[i], lam_init)
        x1, h2, logits = _oproj(o_a, o_b, o_c, x, g_a, w_o[i].astype(BF16), ln1_g[i], ln1_b[i], sc_f, sh_f,
                                rw, alpha)

        h2 = h2.reshape(n, d)
        wt, lc, ki = _router(logits.reshape(n, LANES), router_bias[i])
        plan = _moe_plan(lc, ki, n)
        y_tok = _moe(h2, plan, exp_w_gate, exp_w_up, exp_w_down, i)
        x = _combine(y_tok, wt, h2, x1.reshape(n, d), g_f, sh_w_gate[i].astype(BF16), sh_w_up[i].astype(BF16),
                     sh_w_down[i].astype(BF16), ln2_g[i], ln2_b[i], alpha, s).reshape(b, s, d)
    return x
```

```python
import functools
import math

import numpy as np
import jax
import jax.numpy as jnp
from jax import lax
from jax.experimental import pallas as pl
from jax.experimental.pallas import tpu as pltpu

F32 = jnp.float32
BF16 = jnp.bfloat16

GRID_W = 64
ROPE_THETA = 500000.0
MLA_HEADS, MLA_Q_RANK, MLA_KV_RANK, MLA_NOPE, MLA_ROPE, MLA_V = 8, 512, 256, 128, 64, 128
NA_HEADS, NA_DIM, NA_WIN_H, NA_WIN_W = 4, 128, 8, 16
DIFF_HEADS, DIFF_QK = 4, 64
DIFF_V = 2 * DIFF_QK
DIFF_ROT = DIFF_QK // 4
IN_A = MLA_Q_RANK + MLA_KV_RANK + MLA_ROPE
IN_B = 3 * NA_HEADS * NA_DIM
N_EXPERTS, TOP_K, N_GROUPS, TOPK_GROUPS = 64, 8, 8, 4
ROUTED_SCALE = 2.5
LN_EPS = 1e-5
RMS_EPS = 1e-6

LANES = 128
MLA_HEAD_PAD = 256
VMEM_LIMIT = 56 * 1024 * 1024
ROW_TILE = 512
MLA_Q_TILE = 1024
DIFF_Q_TILE = 512
ATTN_KV_CHUNK = 256
NA_PAIR = 2 * GRID_W
NA_KEYS = 10 * GRID_W
MOE_TILE = 256
ROW_GROUP = 8
TOKEN_BLOCK = LANES
SLOT_GROUP = 8
PREFIX_RADIX = 256.0
COMBINE_TILE = 128
ROUTER_TILE = 512


def _params(sem, vmem=VMEM_LIMIT):
    return pltpu.CompilerParams(dimension_semantics=sem, vmem_limit_bytes=vmem)


def _sigmoid(x):
    return 1.0 / (1.0 + jnp.exp(-x))


def _silu(x):
    return x * _sigmoid(x)


def _ln(x):
    mu = jnp.mean(x, axis=-1, keepdims=True)
    xc = x - mu
    var = jnp.mean(xc * xc, axis=-1, keepdims=True)
    return xc * lax.rsqrt(var + LN_EPS)


def _rms(x, g):
    return x * lax.rsqrt(jnp.mean(x * x, axis=-1, keepdims=True) + RMS_EPS) * g


def _bdot(a, b):
    return jnp.dot(a, b, preferred_element_type=F32)


def _split(a):
    hi = a.astype(BF16)
    lo = (a - hi.astype(F32)).astype(BF16)
    return hi, lo


def _dot3(a, b):
    ah, al = _split(a)
    bh, bl = _split(b)
    return _bdot(ah, bh) + (_bdot(al, bh) + _bdot(ah, bl))


def _rope(x, c, s1, s2, half):
    width = x.shape[1]
    reps = width // c.shape[1]
    if reps > 1:
        c, s1, s2 = (jnp.concatenate([t] * reps, axis=1) for t in (c, s1, s2))
    return x * c + pltpu.roll(x, width - half, 1) * s1 + pltpu.roll(x, half, 1) * s2


def _ada_kernel(c_ref, w_ref, b_ref, o_ref):
    o_ref[0] = _dot3(_silu(c_ref[...]), w_ref[0]) + b_ref[0]


def _ada(c, w_ada, b_ada):
    depth, d, d6 = w_ada.shape
    batch = c.shape[0]
    rows = 8
    cp = jnp.zeros((rows, d), F32).at[:batch].set(c)
    tn = 512
    out = pl.pallas_call(
        _ada_kernel,
        grid=(depth, d6 // tn),
        in_specs=[pl.BlockSpec((rows, d), lambda l, n: (0, 0)),
                  pl.BlockSpec((1, d, tn), lambda l, n: (l, 0, n)),
                  pl.BlockSpec((1, 1, tn), lambda l, n: (l, 0, n))],
        out_specs=pl.BlockSpec((1, rows, tn), lambda l, n: (l, 0, n)),
        out_shape=jax.ShapeDtypeStruct((depth, rows, d6), F32),
        compiler_params=_params(("arbitrary", "arbitrary")),
        name="ada",
    )(cp, w_ada, b_ada.reshape(depth, 1, d6))
    return out[:, :batch]


def _inproj_kernel(x_ref, sc_ref, sh_ref, w_ref, o_ref):
    h = _ln(x_ref[0]) * (1.0 + sc_ref[0]) + sh_ref[0]
    o_ref[0] = _bdot(h.astype(BF16), w_ref[...]).astype(o_ref.dtype)


def _inproj_diff_kernel(x_ref, sc_ref, sh_ref, w_ref, c_ref, s1_ref, s2_ref, qv_ref, kt_ref):
    nq = DIFF_HEADS * 2 * DIFF_QK
    h = _ln(x_ref[0]) * (1.0 + sc_ref[0]) + sh_ref[0]
    acc = _bdot(h.astype(BF16), w_ref[...])
    r = _rope(acc[:, :2 * nq], c_ref[...], s1_ref[...], s2_ref[...], DIFF_ROT // 2)
    qv_ref[0, :, :nq] = (r[:, :nq] * (DIFF_QK ** -0.5)).astype(BF16)
    qv_ref[0, :, nq:] = acc[:, 2 * nq:].astype(BF16)
    for hd in range(DIFF_HEADS):
        kt_ref[0, hd] = r[:, nq + hd * DIFF_V:nq + (hd + 1) * DIFF_V].T.astype(BF16)


def _inproj_specs(x, w):
    b, s, d = x.shape
    tm = min(ROW_TILE, s)
    specs = [pl.BlockSpec((1, tm, d), lambda bi, m: (bi, m, 0)),
             pl.BlockSpec((1, 1, d), lambda bi, m: (bi, 0, 0)),
             pl.BlockSpec((1, 1, d), lambda bi, m: (bi, 0, 0)),
             pl.BlockSpec(w.shape, lambda bi, m: (0, 0))]
    return tm, specs


def _inproj(x, sc, sh, w, out_dtype, name):
    b, s, _ = x.shape
    n = w.shape[1]
    tm, in_specs = _inproj_specs(x, w)
    return pl.pallas_call(
        _inproj_kernel,
        grid=(b, s // tm),
        in_specs=in_specs,
        out_specs=pl.BlockSpec((1, tm, n), lambda bi, m: (bi, m, 0)),
        out_shape=jax.ShapeDtypeStruct((b, s, n), out_dtype),
        compiler_params=_params(("arbitrary", "arbitrary")),
        name=name,
    )(x, sc, sh, w)


def _inproj_diff(x, sc, sh, w, tables):
    b, s, _ = x.shape
    tm, in_specs = _inproj_specs(x, w)
    n_qv = 2 * DIFF_HEADS * DIFF_V
    in_specs += [pl.BlockSpec((tm, LANES), lambda bi, m: (m, 0))] * 3
    return pl.pallas_call(
        _inproj_diff_kernel,
        grid=(b, s // tm),
        in_specs=in_specs,
        out_specs=[pl.BlockSpec((1, tm, n_qv), lambda bi, m: (bi, m, 0)),
                   pl.BlockSpec((1, DIFF_HEADS, DIFF_V, tm), lambda bi, m: (bi, 0, 0, m))],
        out_shape=[jax.ShapeDtypeStruct((b, s, n_qv), BF16),
                   jax.ShapeDtypeStruct((b, DIFF_HEADS, DIFF_V, s), BF16)],
        compiler_params=_params(("arbitrary", "arbitrary")),
        name="inproj_diff",
    )(x, sc, sh, w, *tables)


def _mla_proj_kernel(ua_ref, qn_ref, kvn_ref, wq_ref, wkv_ref, cq_ref, sq1_ref, sq2_ref,
                     ck_ref, sk1_ref, sk2_ref, q_ref, kt_ref, v_ref):
    ua = ua_ref[0]
    kv_lo = MLA_Q_RANK + MLA_KV_RANK
    cq = _rms(ua[:, :MLA_Q_RANK], qn_ref[...])
    q = _bdot(cq.astype(BF16), wq_ref[...])
    q = _rope(q, cq_ref[...], sq1_ref[...], sq2_ref[...], MLA_ROPE // 2)
    q_ref[0] = (q * ((MLA_NOPE + MLA_ROPE) ** -0.5)).astype(BF16)
    ckv = _rms(ua[:, MLA_Q_RANK:kv_lo], kvn_ref[...])
    kv = _bdot(ckv.astype(BF16), wkv_ref[...])
    nk = MLA_HEADS * MLA_NOPE
    v_ref[0] = kv[:, nk:].astype(BF16)
    kr = _rope(ua[:, kv_lo:], ck_ref[...], sk1_ref[...], sk2_ref[...], MLA_ROPE // 2)
    krt = kr.T.astype(BF16)
    for h in range(MLA_HEADS):
        kt_ref[0, h, :MLA_NOPE, :] = kv[:, h * MLA_NOPE:(h + 1) * MLA_NOPE].T.astype(BF16)
        kt_ref[0, h, MLA_NOPE:, :] = krt


def _mla_proj(ua, q_norm, kv_norm, wq, wkv, tq, tk):
    b, s, wa = ua.shape
    tm = min(ROW_TILE, s)
    hq = MLA_HEADS * MLA_HEAD_PAD
    hv = MLA_HEADS * MLA_V
    const = lambda bi, m: (0, 0)
    tab = lambda w: pl.BlockSpec((tm, w), lambda bi, m: (m, 0))
    return pl.pallas_call(
        _mla_proj_kernel,
        grid=(b, s // tm),
        in_specs=[pl.BlockSpec((1, tm, wa), lambda bi, m: (bi, m, 0)),
                  pl.BlockSpec((1, MLA_Q_RANK), const),
                  pl.BlockSpec((1, MLA_KV_RANK), const),
                  pl.BlockSpec(wq.shape, const),
                  pl.BlockSpec(wkv.shape, const),
                  tab(MLA_HEAD_PAD), tab(MLA_HEAD_PAD), tab(MLA_HEAD_PAD),
                  tab(LANES), tab(LANES), tab(LANES)],
        out_specs=[pl.BlockSpec((1, tm, hq), lambda bi, m: (bi, m, 0)),
                   pl.BlockSpec((1, MLA_HEADS, MLA_HEAD_PAD, tm), lambda bi, m: (bi, 0, 0, m)),
                   pl.BlockSpec((1, tm, hv), lambda bi, m: (bi, m, 0))],
        out_shape=[jax.ShapeDtypeStruct((b, s, hq), BF16),
                   jax.ShapeDtypeStruct((b, MLA_HEADS, MLA_HEAD_PAD, s), BF16),
                   jax.ShapeDtypeStruct((b, s, hv), BF16)],
        compiler_params=_params(("arbitrary", "arbitrary")),
        name="mla_proj",
    )(ua, q_norm.reshape(1, -1), kv_norm.reshape(1, -1), wq, wkv, *tq, *tk)


class _Softmax:
    def __init__(self):
        self.m = self.acc = None

    def update(self, sc, v_ones):
        mc = jnp.max(sc, axis=1, keepdims=True)
        m_new = mc if self.m is None else jnp.maximum(self.m, mc)
        pv = _bdot(jnp.exp(sc - m_new).astype(BF16), v_ones)
        self.acc = pv if self.m is None else jnp.exp(self.m - m_new) * self.acc + pv
        self.m = m_new

    def result(self):
        dv = self.acc.shape[1] - LANES
        return self.acc[:, :dv] * (1.0 / self.acc[:, dv:dv + 1])


def _kv_chunks(s):
    kc = min(ATTN_KV_CHUNK, s)
    return [(c * kc, (c + 1) * kc) for c in range(s // kc)]


def _with_ones(v):
    return jnp.concatenate([v, jnp.ones((v.shape[0], LANES), v.dtype)], axis=1)


def _mla_attn_kernel(q_ref, kt_ref, v_ref, o_ref):
    q = q_ref[0]
    sm = _Softmax()
    for lo, hi in _kv_chunks(kt_ref.shape[3]):
        sm.update(_bdot(q, kt_ref[0, 0, :, lo:hi]), _with_ones(v_ref[0, lo:hi, :]))
    o_ref[0] = sm.result().astype(BF16)


def _mla_attn(q, kt, v):
    b, s, _ = q.shape
    tq = min(MLA_Q_TILE, s)
    return pl.pallas_call(
        _mla_attn_kernel,
        grid=(b, MLA_HEADS, s // tq),
        in_specs=[pl.BlockSpec((1, tq, MLA_HEAD_PAD), lambda bi, h, i: (bi, i, h)),
                  pl.BlockSpec((1, 1, MLA_HEAD_PAD, s), lambda bi, h, i: (bi, h, 0, 0)),
                  pl.BlockSpec((1, s, MLA_V), lambda bi, h, i: (bi, 0, h))],
        out_specs=pl.BlockSpec((1, tq, MLA_V), lambda bi, h, i: (bi, i, h)),
        out_shape=jax.ShapeDtypeStruct((b, s, MLA_HEADS * MLA_V), BF16),
        compiler_params=_params(("arbitrary", "arbitrary", "arbitrary")),
        name="mla_attn",
    )(q, kt, v)


def _diff_attn_kernel(q_ref, kt_ref, v_ref, lq1_ref, lk1_ref, lq2_ref, lk2_ref, sub_ref, o_ref, *, lam_init):
    q = q_ref[0]
    lane = lax.broadcasted_iota(jnp.int32, q.shape, 1)
    zero = jnp.zeros_like(q)
    q1 = jnp.where(lane < DIFF_QK, q, zero)
    q2 = jnp.where(lane >= DIFF_QK, q, zero)
    lam = (jnp.exp(jnp.sum(lq1_ref[...] * lk1_ref[...], axis=1, keepdims=True))
           - jnp.exp(jnp.sum(lq2_ref[...] * lk2_ref[...], axis=1, keepdims=True)) + lam_init)
    sm1, sm2 = _Softmax(), _Softmax()
    for lo, hi in _kv_chunks(kt_ref.shape[3]):
        kt = kt_ref[0, 0, :, lo:hi]
        v = _with_ones(v_ref[0, lo:hi, :])
        sm1.update(_bdot(q1, kt), v)
        sm2.update(_bdot(q2, kt), v)
    o = sm1.result() - lam * sm2.result()
    o_ref[0] = (_rms(o, sub_ref[...]) * (1.0 - lam_init)).astype(BF16)


def _diff_attn(uc, kt, lq1, lk1, lq2, lk2, subln, lam_init):
    b, s, _ = uc.shape
    tq = min(DIFF_Q_TILE, s)
    vec = lambda n: pl.BlockSpec((1, n), lambda bi, h, i: (0, 0))
    v_block0 = DIFF_HEADS
    return pl.pallas_call(
        functools.partial(_diff_attn_kernel, lam_init=lam_init),
        grid=(b, DIFF_HEADS, s // tq),
        in_specs=[pl.BlockSpec((1, tq, DIFF_V), lambda bi, h, i: (bi, i, h)),
                  pl.BlockSpec((1, 1, 2 * DIFF_QK, s), lambda bi, h, i: (bi, h, 0, 0)),
                  pl.BlockSpec((1, s, DIFF_V), lambda bi, h, i: (bi, 0, v_block0 + h)),
                  vec(DIFF_QK), vec(DIFF_QK), vec(DIFF_QK), vec(DIFF_QK), vec(DIFF_V)],
        out_specs=pl.BlockSpec((1, tq, DIFF_V), lambda bi, h, i: (bi, i, h)),
        out_shape=jax.ShapeDtypeStruct((b, s, DIFF_HEADS * DIFF_V), BF16),
        compiler_params=_params(("arbitrary", "arbitrary", "arbitrary")),
        name="diff_attn",
    )(uc, kt, uc, lq1.reshape(1, -1), lk1.reshape(1, -1), lq2.reshape(1, -1), lk2.reshape(1, -1),
      subln.reshape(1, -1))


def _na_plan(rows):
    kh = min(NA_WIN_H, rows)
    key_rows = NA_KEYS // GRID_W
    r0 = lambda r: min(max(r - kh // 2, 0), rows - kh)
    starts, cases, sigs = [], [], []
    for p in range(rows // 2):
        sr = min(r0(2 * p), rows - key_rows)
        sig = (r0(2 * p) - 2 * p, r0(2 * p + 1) - 2 * p - 1, sr - 2 * p)
        if sig not in sigs:
            sigs.append(sig)
        starts.append(sr)
        cases.append(sigs.index(sig))
    return np.array(starts, np.int32), np.array(cases, np.int32), sigs, kh


def _na_bias(rpb, rows):
    _, _, sigs, kh = _na_plan(rows)
    j = np.arange(NA_PAIR)
    m = np.arange(NA_KEYS)
    qrow_rel, qcol = j // GRID_W, j % GRID_W
    krow_rel, kcol = m // GRID_W, m % GRID_W
    c0 = np.clip(qcol - NA_WIN_W // 2, 0, GRID_W - NA_WIN_W)
    col_ok = (kcol[None, :] >= c0[:, None]) & (kcol[None, :] < c0[:, None] + NA_WIN_W)
    edge = GRID_W - NA_WIN_W
    ext = jnp.pad(rpb, ((0, 0), (0, 0), (edge, edge)), mode="edge")
    toe = jnp.stack([ext[:, :, GRID_W - 1 - qc:2 * GRID_W - 1 - qc] for qc in range(GRID_W)], axis=2)
    n_dr = 2 * NA_WIN_H - 1
    key_rows = NA_KEYS // GRID_W
    tabs = []
    for (d0, d1, dsr) in sigs:
        r0_rel = np.where(qrow_rel == 0, d0, d1 + 1)
        krel = dsr + krow_rel
        ok = (krel[None, :] >= r0_rel[:, None]) & (krel[None, :] < r0_rel[:, None] + kh) & col_ok
        blocks = jnp.stack(
            [jnp.stack([toe[:, min(max(dsr + kr - qr + NA_WIN_H - 1, 0), n_dr - 1)] for kr in range(key_rows)],
                       axis=2) for qr in range(2)], axis=1)
        bias = blocks.reshape(rpb.shape[0], NA_PAIR, NA_KEYS)
        tabs.append(jnp.where(ok[None], bias, -jnp.inf))
    return jnp.stack(tabs, axis=1).astype(F32)


def _na_kernel(start_ref, case_ref, q_ref, k_ref, v_ref, bias_ref, o_ref):
    def body(p, carry):
        qs = pl.ds(pl.multiple_of(p * NA_PAIR, NA_PAIR), NA_PAIR)
        ks = pl.ds(pl.multiple_of(start_ref[p] * GRID_W, LANES), NA_KEYS)
        q = q_ref[0, qs, :]
        sc = lax.dot_general(q, k_ref[0, ks, :], (((1,), (1,)), ((), ())), preferred_element_type=F32)
        sc = sc * (NA_DIM ** -0.5) + bias_ref[0, case_ref[p]]
        e = jnp.exp(sc - jnp.max(sc, axis=1, keepdims=True))
        pr = e / jnp.sum(e, axis=1, keepdims=True)
        o_ref[0, qs, :] = _bdot(pr.astype(BF16), v_ref[0, ks, :]).astype(BF16)
        return carry

    lax.fori_loop(0, q_ref.shape[1] // NA_PAIR, body, 0, unroll=8)


def _na_attn(ub, bias):
    b, s, _ = ub.shape
    starts, cases, sigs, _ = _na_plan(s // GRID_W)
    blk = lambda off: pl.BlockSpec((1, s, NA_DIM), lambda bi, h, st, cs: (bi, 0, off + h))
    return pl.pallas_call(
        _na_kernel,
        grid_spec=pltpu.PrefetchScalarGridSpec(
            num_scalar_prefetch=2,
            grid=(b, NA_HEADS),
            in_specs=[blk(0), blk(NA_HEADS), blk(2 * NA_HEADS),
                      pl.BlockSpec((1, len(sigs), NA_PAIR, NA_KEYS), lambda bi, h, st, cs: (h, 0, 0, 0))],
            out_specs=pl.BlockSpec((1, s, NA_DIM), lambda bi, h, st, cs: (bi, 0, h)),
        ),
        out_shape=jax.ShapeDtypeStruct((b, s, NA_HEADS * NA_DIM), BF16),
        compiler_params=_params(("arbitrary", "arbitrary")),
        name="na_attn",
    )(jnp.asarray(starts), jnp.asarray(cases), ub, ub, ub, bias)


def _oproj_kernel(oa_ref, ob_ref, oc_ref, x_ref, ga_ref, wo_ref, g1_ref, b1_ref, scf_ref, shf_ref, rw_ref,
                  x1_ref, h2_ref, lg_ref, *, alpha):
    na, nb = oa_ref.shape[2], ob_ref.shape[2]
    acc = (_bdot(oa_ref[0], wo_ref[:na, :]) + _bdot(ob_ref[0], wo_ref[na:na + nb, :])
           + _bdot(oc_ref[0], wo_ref[na + nb:, :]))
    x1 = _ln(alpha * x_ref[0] + (1.0 + ga_ref[0]) * acc) * g1_ref[...] + b1_ref[...]
    x1_ref[0] = x1
    h2 = _ln(x1) * (1.0 + scf_ref[0]) + shf_ref[0]
    h2_ref[0] = h2
    lg_ref[0] = _dot3(h2, rw_ref[...])


def _oproj(oa, ob, oc, x, ga, wo, g1, b1, scf, shf, rw, alpha):
    b, s, d = x.shape
    tm = min(ROW_TILE // 2, s)
    row = lambda w: pl.BlockSpec((1, tm, w), lambda bi, m: (bi, m, 0))
    per_b = pl.BlockSpec((1, 1, d), lambda bi, m: (bi, 0, 0))
    const = lambda shape: pl.BlockSpec(shape, lambda bi, m: (0, 0))
    return pl.pallas_call(
        functools.partial(_oproj_kernel, alpha=alpha),
        grid=(b, s // tm),
        in_specs=[row(oa.shape[2]), row(ob.shape[2]), row(oc.shape[2]), row(d), per_b,
                  const(wo.shape), const((1, d)), const((1, d)), per_b, per_b, const(rw.shape)],
        out_specs=[row(d), row(d), row(LANES)],
        out_shape=[jax.ShapeDtypeStruct((b, s, d), F32), jax.ShapeDtypeStruct((b, s, d), F32),
                   jax.ShapeDtypeStruct((b, s, LANES), F32)],
        compiler_params=_params(("arbitrary", "arbitrary")),
        name="oproj",
    )(oa, ob, oc, x, ga, wo, g1.reshape(1, d), b1.reshape(1, d), scf, shf, rw)


def _first_max(cur, idx, axis, size):
    m = jnp.max(cur, axis=axis, keepdims=True)
    first = jnp.min(jnp.where(cur == m, idx, size), axis=axis, keepdims=True)
    return idx == first, m


def _router_kernel(lg_ref, rb_ref, wt_ref, lc_ref, ki_ref):
    tn = lg_ref.shape[0]
    gsz = N_EXPERTS // N_GROUPS
    neg = -jnp.inf
    scores = _sigmoid(lg_ref[...].T[:N_EXPERTS, :])
    biased = scores + rb_ref[...]
    b3 = biased.reshape(N_GROUPS, gsz, tn)
    i3 = lax.broadcasted_iota(jnp.int32, b3.shape, 1)
    pick1, m1 = _first_max(b3, i3, 1, gsz)
    m2 = jnp.max(jnp.where(pick1, neg, b3), axis=1, keepdims=True)
    gs = (m1 + m2).reshape(N_GROUPS, tn)
    gi = lax.broadcasted_iota(jnp.int32, gs.shape, 0)
    gsel = jnp.zeros(gs.shape, F32)
    for _ in range(TOPK_GROUPS):
        pick, _m = _first_max(gs, gi, 0, N_GROUPS)
        gsel = jnp.where(pick, 1.0, gsel)
        gs = jnp.where(pick, neg, gs)
    emask = jnp.broadcast_to(gsel.reshape(N_GROUPS, 1, tn), b3.shape).reshape(N_EXPERTS, tn)
    cur = jnp.where(emask > 0.0, biased, neg)
    ei = lax.broadcasted_iota(jnp.int32, cur.shape, 0)
    self_ = jnp.zeros(cur.shape, F32)
    for _ in range(TOP_K):
        pick, _m = _first_max(cur, ei, 0, N_EXPERTS)
        self_ = jnp.where(pick, 1.0, self_)
        cur = jnp.where(pick, neg, cur)
    sel = self_ > 0.0
    w = jnp.where(sel, scores, 0.0)
    gate = w / jnp.sum(w, axis=0, keepdims=True) * ROUTED_SCALE
    selb = self_.astype(BF16)
    lower = (lax.broadcasted_iota(jnp.int32, (N_EXPERTS, N_EXPERTS), 1)
             < lax.broadcasted_iota(jnp.int32, (N_EXPERTS, N_EXPERTS), 0)).astype(BF16)
    kidx = _bdot(lower, selb)
    ki_ref[...] = kidx.astype(BF16)
    ti = lax.broadcasted_iota(jnp.int32, (tn, tn), 0)
    tj = lax.broadcasted_iota(jnp.int32, (tn, tn), 1)
    same_block = (ti // TOKEN_BLOCK) == (tj // TOKEN_BLOCK)
    lc_ref[...] = _bdot(selb, jnp.logical_and(ti <= tj, same_block).astype(BF16)).astype(BF16)
    w_rows = [jnp.sum(jnp.where(sel & (kidx == float(k)), gate, 0.0), axis=0, keepdims=True)
              for k in range(TOP_K)]
    wk = jnp.concatenate(w_rows + [jnp.zeros((LANES - TOP_K, tn), F32)], axis=0)
    wt_ref[...] = wk.T


def _router(logits, router_bias):
    n = logits.shape[0]
    tn = min(ROUTER_TILE, n)
    return pl.pallas_call(
        _router_kernel,
        grid=(n // tn,),
        in_specs=[pl.BlockSpec((tn, LANES), lambda t: (t, 0)),
                  pl.BlockSpec((N_EXPERTS, 1), lambda t: (0, 0))],
        out_specs=[pl.BlockSpec((tn, LANES), lambda t: (t, 0)),
                   pl.BlockSpec((N_EXPERTS, tn), lambda t: (0, t)),
                   pl.BlockSpec((N_EXPERTS, tn), lambda t: (0, t))],
        out_shape=[jax.ShapeDtypeStruct((n, LANES), F32),
                   jax.ShapeDtypeStruct((N_EXPERTS, n), BF16),
                   jax.ShapeDtypeStruct((N_EXPERTS, n), BF16)],
        compiler_params=_params(("arbitrary",)),
        name="router",
    )(logits, router_bias.reshape(N_EXPERTS, 1))


def _slots_kernel(te_ref, r0_ref, lc_ref, ki_ref, pin_ref, pexh_ref, pexl_ref, src_ref, dst_ref, *, n):
    g = pl.program_id(0)
    tm = src_ref.shape[2]
    nb = lc_ref.shape[1]
    slot_i = lax.broadcasted_iota(jnp.int32, (tm, 1), 0).astype(F32)
    blk_i = lax.broadcasted_iota(jnp.int32, (tm, nb), 1).astype(F32)
    lane_i = lax.broadcasted_iota(jnp.int32, (tm, TOKEN_BLOCK), 1).astype(F32)
    ones_nb = jnp.ones((nb, LANES), BF16)
    ones_tb = jnp.ones((TOKEN_BLOCK, LANES), BF16)

    def lanes(x):
        return x.T[0:1, :].astype(jnp.int32)

    for j in range(SLOT_GROUP):
        i = g * SLOT_GROUP + j
        e = te_ref[i]
        r = slot_i + r0_ref[i].astype(F32)
        blk = _bdot((pin_ref[e] <= r).astype(BF16), ones_nb)
        ohb = (blk_i == blk[:, :nb]).astype(BF16)
        hi = _bdot(ohb, jnp.broadcast_to(pexh_ref[e], (nb, LANES)).astype(BF16))
        lo = _bdot(ohb, jnp.broadcast_to(pexl_ref[e], (nb, LANES)).astype(BF16))
        r_loc = r - (hi * PREFIX_RADIX + lo)
        tl = _bdot((_bdot(ohb, lc_ref[e]) <= r_loc).astype(BF16), ones_tb)
        kk = _bdot(jnp.where(lane_i == tl, _bdot(ohb, ki_ref[e]), 0.0).astype(BF16), ones_tb)
        tok = blk * TOKEN_BLOCK + tl
        valid = blk < nb
        src_ref[j] = lanes(jnp.where(valid, tok, 0.0))
        dst_ref[j] = lanes(jnp.where(valid, kk * n + tok, 0.0))


def _moe_plan(lc, ki, n):
    tm = MOE_TILE
    n_tiles = TOP_K * n // tm + N_EXPERTS
    nb = n // TOKEN_BLOCK
    lc3 = lc.reshape(N_EXPERTS, nb, TOKEN_BLOCK)
    cb = lc3[:, :, TOKEN_BLOCK - 1].astype(F32)
    pin = jnp.cumsum(cb, axis=1)
    pex = pin - cb
    pex_hi = jnp.floor(pex / PREFIX_RADIX)
    counts = pin[:, nb - 1].astype(jnp.int32)
    ntile = (counts + tm - 1) // tm
    tile_end = jnp.cumsum(ntile)
    tile_start = tile_end - ntile
    n_used = tile_end[-1]
    ti = jnp.minimum(jnp.arange(n_tiles, dtype=jnp.int32), n_used - 1)
    te = jnp.minimum(jnp.sum(tile_end[None, :] <= ti[:, None], axis=1), N_EXPERTS - 1).astype(jnp.int32)
    onehot_te = te[:, None] == jnp.arange(N_EXPERTS, dtype=jnp.int32)[None, :]
    start_te = jnp.sum(jnp.where(onehot_te, tile_start[None, :], 0), axis=1)
    count_te = jnp.sum(jnp.where(onehot_te, counts[None, :], 0), axis=1)
    first = (jnp.arange(n_tiles, dtype=jnp.int32) == start_te).astype(jnp.int32)
    r0 = ((ti - start_te) * tm).astype(jnp.int32)
    nvalid = jnp.clip(count_te - r0, 0, tm).astype(jnp.int32)
    ar = jnp.arange(N_EXPERTS, dtype=jnp.int32)
    later_used = jnp.logical_and(ar[None, :] > ar[:, None], (counts > 0)[None, :])
    next_used = jnp.min(jnp.where(later_used, ar[None, :], N_EXPERTS), axis=1)
    nxt_e = jnp.sum(jnp.where(onehot_te, next_used[None, :], 0), axis=1).astype(jnp.int32)
    whole = lambda shape: pl.BlockSpec(shape, lambda g, *_: (0,) * len(shape))
    out_blk = pl.BlockSpec((SLOT_GROUP, 1, tm), lambda g, *_: (g, 0, 0))
    src, dst = pl.pallas_call(
        functools.partial(_slots_kernel, n=n),
        grid_spec=pltpu.PrefetchScalarGridSpec(
            num_scalar_prefetch=2,
            grid=(n_tiles // SLOT_GROUP,),
            in_specs=[whole((N_EXPERTS, nb, TOKEN_BLOCK)), whole((N_EXPERTS, nb, TOKEN_BLOCK)),
                      whole((N_EXPERTS, 1, nb)), whole((N_EXPERTS, nb, 1)), whole((N_EXPERTS, nb, 1))],
            out_specs=[out_blk, out_blk],
        ),
        out_shape=[jax.ShapeDtypeStruct((n_tiles, 1, tm), jnp.int32)] * 2,
        compiler_params=_params(("arbitrary",)),
        name="moe_slots",
    )(te, r0, lc3, ki.reshape(N_EXPERTS, nb, TOKEN_BLOCK), pin.reshape(N_EXPERTS, 1, nb),
      pex_hi.reshape(N_EXPERTS, nb, 1), (pex - pex_hi * PREFIX_RADIX).reshape(N_EXPERTS, nb, 1))
    return te, first, nvalid, nxt_e, n_used.reshape(1).astype(jnp.int32), src, dst


def _moe_kernel(te_ref, first_ref, nvalid_ref, nxte_ref, nused_ref, src_cur, src_nxt, dst_cur, dst_prv, h_hbm,
                wg_hbm, wu_hbm, wd_hbm, y_hbm, xbuf0, xbuf1, ybuf0, ybuf1, wgf, wuf, wdf, wgb, wub, wdb,
                gsem, ssem, wsem, *, layer):
    i = pl.program_id(0)
    tm = xbuf0.shape[0]
    n_used = nused_ref[0]
    nv = nvalid_ref[i]
    xbufs, ybufs = (xbuf0, xbuf1), (ybuf0, ybuf1)

    def gather(src_ref, p):
        for r in range(tm):
            pltpu.make_async_copy(h_hbm.at[pl.ds(src_ref[0, 0, r], 1)], xbufs[p].at[pl.ds(r, 1)],
                                  gsem.at[p]).start()

    def gather_wait(p):
        pltpu.make_async_copy(h_hbm.at[pl.ds(0, tm)], xbufs[p], gsem.at[p]).wait()

    def scatter_row(dst_ref, p, r):
        pltpu.make_async_copy(ybufs[p].at[pl.ds(r, 1)], y_hbm.at[pl.ds(dst_ref[0, 0, r], 1)],
                              ssem.at[p]).start()

    def scatter_wait(p, rows):
        @pl.when(rows == tm)
        def _():
            pltpu.make_async_copy(ybufs[p], y_hbm.at[pl.ds(0, tm)], ssem.at[p]).wait()

        @pl.when(rows < tm)
        def _():
            def wait_rows(count):
                def body(r, carry):
                    pltpu.make_async_copy(ybufs[p].at[pl.ds(0, count)], y_hbm.at[pl.ds(0, count)],
                                          ssem.at[p]).wait()
                    return carry
                return body

            lax.fori_loop(0, rows // ROW_GROUP, wait_rows(ROW_GROUP), 0)
            lax.fori_loop(0, rows % ROW_GROUP, wait_rows(1), 0)

    def weight_copies(e):
        return [pltpu.make_async_copy(hbm.at[layer, e], buf, wsem.at[k])
                for k, (hbm, buf) in enumerate(((wg_hbm, wgf), (wu_hbm, wuf), (wd_hbm, wdf)))]

    def step(p):
        @pl.when(i >= 2)
        def _():
            scatter_wait(p, nvalid_ref[i - 2])

        gather_wait(p)

        @pl.when(first_ref[i] == 1)
        def _():
            @pl.when(i == 0)
            def _():
                for cp in weight_copies(te_ref[0]):
                    cp.start(priority=1)

            for cp in weight_copies(te_ref[i]):
                cp.wait()
            wgb[...] = wgf[...].astype(BF16)
            wub[...] = wuf[...].astype(BF16)
            wdb[...] = wdf[...].astype(BF16)
            nxt = nxte_ref[i]

            @pl.when(nxt < N_EXPERTS)
            def _():
                for cp in weight_copies(nxt):
                    cp.start(priority=1)

        def main(scatter_prev):
            gather(src_nxt, 1 - p)
            if scatter_prev:
                for r in range(tm):
                    scatter_row(dst_prv, 1 - p, r)
            x = xbufs[p][...].astype(BF16)
            act = _silu(_bdot(x, wgb[...])) * _bdot(x, wub[...])
            ybufs[p][...] = _bdot(act.astype(BF16), wdb[...])

        prev_full = jnp.logical_and(i >= 1, nvalid_ref[jnp.maximum(i - 1, 0)] == tm)

        @pl.when(prev_full)
        def _():
            main(True)

        @pl.when(jnp.logical_not(prev_full))
        def _():
            main(False)

        last = i == n_used - 1

        @pl.when(jnp.logical_or(nv < tm, last))
        def _():
            def group_body(g, carry):
                for u in range(ROW_GROUP):
                    scatter_row(dst_cur, p, g * ROW_GROUP + u)
                return carry

            def row_body(r, carry):
                scatter_row(dst_cur, p, r)
                return carry

            groups = nv // ROW_GROUP
            lax.fori_loop(0, groups, group_body, 0)
            lax.fori_loop(groups * ROW_GROUP, nv, row_body, 0)

        @pl.when(last)
        def _():
            gather_wait(1 - p)
            scatter_wait(p, nv)

            @pl.when(i >= 1)
            def _():
                scatter_wait(1 - p, nvalid_ref[i - 1])

    @pl.when(i < n_used)
    def _():
        @pl.when(i == 0)
        def _():
            gather(src_cur, 0)

        for p in range(2):
            @pl.when(i % 2 == p)
            def _():
                step(p)


def _moe(h2, plan, w_gate, w_up, w_down, layer):
    n, d = h2.shape
    te, first, nvalid, nxt_e, n_used, src, dst = plan
    n_tiles, _, tm = src.shape
    f = w_gate.shape[3]
    idx_blk = lambda fn: pl.BlockSpec((1, 1, tm), fn, memory_space=pltpu.SMEM)
    hbm = pl.BlockSpec(memory_space=pl.ANY)
    return pl.pallas_call(
        functools.partial(_moe_kernel, layer=layer),
        grid_spec=pltpu.PrefetchScalarGridSpec(
            num_scalar_prefetch=5,
            grid=(n_tiles,),
            in_specs=[idx_blk(lambda i, *_: (i, 0, 0)),
                      idx_blk(lambda i, *_: (jnp.minimum(i + 1, n_tiles - 1), 0, 0)),
                      idx_blk(lambda i, *_: (i, 0, 0)),
                      idx_blk(lambda i, *_: (jnp.maximum(i - 1, 0), 0, 0)),
                      hbm, hbm, hbm, hbm],
            out_specs=hbm,
            scratch_shapes=[pltpu.VMEM((tm, d), F32)] * 4
                           + [pltpu.VMEM((d, f), F32), pltpu.VMEM((d, f), F32), pltpu.VMEM((f, d), F32),
                              pltpu.VMEM((d, f), BF16), pltpu.VMEM((d, f), BF16), pltpu.VMEM((f, d), BF16),
                              pltpu.SemaphoreType.DMA((2,)), pltpu.SemaphoreType.DMA((2,)),
                              pltpu.SemaphoreType.DMA((3,))],
        ),
        out_shape=jax.ShapeDtypeStruct((TOP_K * n, d), F32),
        compiler_params=_params(("arbitrary",)),
        name="moe_experts",
    )(te, first, nvalid, nxt_e, n_used, src, src, dst, dst, h2, w_gate, w_up, w_down)


def _combine_kernel(*refs, alpha):
    y_refs = refs[:TOP_K]
    wt_ref, h_ref, x_ref, gf_ref, sg_ref, su_ref, sd_ref, g2_ref, b2_ref, o_ref = refs[TOP_K:]
    wt = wt_ref[...]
    routed = wt[:, 0:1] * y_refs[0][...]
    for k in range(1, TOP_K):
        routed = routed + wt[:, k:k + 1] * y_refs[k][...]
    hb = h_ref[...].astype(BF16)
    act = _silu(_bdot(hb, sg_ref[...])) * _bdot(hb, su_ref[...])
    y = routed + _bdot(act.astype(BF16), sd_ref[...])
    o_ref[...] = _ln(alpha * x_ref[...] + (1.0 + gf_ref[0]) * y) * g2_ref[...] + b2_ref[...]


def _combine(y_tok, wt, h2, x1, gf, sg, su, sd, g2, b2, alpha, seq):
    n, d = h2.shape
    tm = min(COMBINE_TILE, seq)
    nblk = n // tm
    row = pl.BlockSpec((tm, d), lambda i: (i, 0))
    const = lambda shape: pl.BlockSpec(shape, lambda i: (0, 0))
    y_specs = [pl.BlockSpec((tm, d), (lambda i, k=k: (k * nblk + i, 0))) for k in range(TOP_K)]
    return pl.pallas_call(
        functools.partial(_combine_kernel, alpha=alpha),
        grid=(nblk,),
        in_specs=y_specs + [pl.BlockSpec((tm, LANES), lambda i: (i, 0)), row, row,
                            pl.BlockSpec((1, 1, d), lambda i: (i // (seq // tm), 0, 0)),
                            const(sg.shape), const(su.shape), const(sd.shape), const((1, d)), const((1, d))],
        out_specs=row,
        out_shape=jax.ShapeDtypeStruct((n, d), F32),
        compiler_params=_params(("arbitrary",)),
        name="combine",
    )(*([y_tok] * TOP_K), wt, h2, x1, gf, sg, su, sd, g2.reshape(1, d), b2.reshape(1, d))


def _rope_tables(s, half, width, offset, period=None):
    inv = ROPE_THETA ** (-jnp.arange(half, dtype=F32) / half)
    ang = jnp.arange(s, dtype=jnp.int32).astype(F32)[:, None] * inv[None, :]
    cos, sin = jnp.cos(ang), jnp.sin(ang)
    period = period or width
    c = jnp.ones((s, period), F32).at[:, offset:offset + half].set(cos)
    c = c.at[:, offset + half:offset + 2 * half].set(cos)
    s1 = jnp.zeros((s, period), F32).at[:, offset:offset + half].set(-sin)
    s2 = jnp.zeros((s, period), F32).at[:, offset + half:offset + 2 * half].set(sin)
    reps = width // period
    return tuple(jnp.tile(t, (1, reps)) for t in (c, s1, s2))


def kernel(x, c, w_ada, b_ada, w_in, mla_q_norm, mla_w_uq, mla_kv_norm, mla_w_ukv, na_rpb, diff_lq1, diff_lk1,
           diff_lq2, diff_lk2, diff_subln, w_o, ln1_g, ln1_b, router_w, router_bias, exp_w_gate, exp_w_up,
           exp_w_down, sh_w_gate, sh_w_up, sh_w_down, ln2_g, ln2_b):
    b, s, d = x.shape
    depth = w_ada.shape[0]
    n = b * s
    alpha = (2 * depth) ** 0.25
    ada = _ada(c, w_ada, b_ada)
    tab_diff = _rope_tables(s, DIFF_ROT // 2, LANES, 0, period=DIFF_QK)
    tab_q = _rope_tables(s, MLA_ROPE // 2, MLA_HEAD_PAD, MLA_NOPE)
    tab_k = _rope_tables(s, MLA_ROPE // 2, LANES, 0)
    a_pad = LANES - MLA_ROPE

    for i in range(depth):
        lam_init = 0.8 - 0.6 * math.exp(-0.3 * i)
        sh_a, sc_a, g_a, sh_f, sc_f, g_f = [a[:, None, :] for a in jnp.split(ada[i], 6, -1)]
        wi = w_in[i]
        w_a = jnp.pad(wi[:, :IN_A], ((0, 0), (0, a_pad))).astype(BF16)
        w_b = wi[:, IN_A:IN_A + IN_B].astype(BF16)
        w_c = wi[:, IN_A + IN_B:].astype(BF16)
        wq = jnp.pad(mla_w_uq[i].reshape(MLA_Q_RANK, MLA_HEADS, MLA_NOPE + MLA_ROPE),
                     ((0, 0), (0, 0), (0, MLA_HEAD_PAD - MLA_NOPE - MLA_ROPE)))
        wq = wq.reshape(MLA_Q_RANK, MLA_HEADS * MLA_HEAD_PAD).astype(BF16)
        wkv3 = mla_w_ukv[i].reshape(MLA_KV_RANK, MLA_HEADS, MLA_NOPE + MLA_V)
        wkv = jnp.concatenate([wkv3[:, :, :MLA_NOPE].reshape(MLA_KV_RANK, -1),
                               wkv3[:, :, MLA_NOPE:].reshape(MLA_KV_RANK, -1)], axis=1).astype(BF16)
        rw = jnp.pad(router_w[i], ((0, 0), (0, LANES - N_EXPERTS)))

        ua = _inproj(x, sc_a, sh_a, w_a, F32, "inproj_mla")
        ub = _inproj(x, sc_a, sh_a, w_b, BF16, "inproj_na")
        uc, kt_c = _inproj_diff(x, sc_a, sh_a, w_c, tab_diff)
        q_a, kt_a, v_a = _mla_proj(ua, mla_q_norm[i], mla_kv_norm[i], wq, wkv, tab_q, tab_k)
        o_a = _mla_attn(q_a, kt_a, v_a)
        o_b = _na_attn(ub, _na_bias(na_rpb[i], s // GRID_W))
        o_c = _diff_attn(uc, kt_c, diff_lq1[i], diff_lk1[i], diff_lq2[i], diff_lk2[i], diff_subln[i], lam_init)
        x1, h2, logits = _oproj(o_a, o_b, o_c, x, g_a, w_o[i].astype(BF16), ln1_g[i], ln1_b[i], sc_f, sh_f,
                                rw, alpha)

        h2 = h2.reshape(n, d)
        wt, lc, ki = _router(logits.reshape(n, LANES), router_bias[i])
        plan = _moe_plan(lc, ki, n)
        y_tok = _moe(h2, plan, exp_w_gate, exp_w_up, exp_w_down, i)
        x = _combine(y_tok, wt, h2, x1.reshape(n, d), g_f, sh_w_gate[i].astype(BF16), sh_w_up[i].astype(BF16),
                     sh_w_down[i].astype(BF16), ln2_g[i], ln2_b[i], alpha, s).reshape(b, s, d)
    return x
```

```python
import functools
import math

import numpy as np
import jax
import jax.numpy as jnp
from jax import lax
from jax.experimental import pallas as pl
from jax.experimental.pallas import tpu as pltpu

F32 = jnp.float32
BF16 = jnp.bfloat16

GRID_W = 64
ROPE_THETA = 500000.0
MLA_HEADS, MLA_Q_RANK, MLA_KV_RANK, MLA_NOPE, MLA_ROPE, MLA_V = 8, 512, 256, 128, 64, 128
NA_HEADS, NA_DIM, NA_WIN_H, NA_WIN_W = 4, 128, 8, 16
DIFF_HEADS, DIFF_QK = 4, 64
DIFF_V = 2 * DIFF_QK
DIFF_ROT = DIFF_QK // 4
IN_A = MLA_Q_RANK + MLA_KV_RANK + MLA_ROPE
IN_B = 3 * NA_HEADS * NA_DIM
N_EXPERTS, TOP_K, N_GROUPS, TOPK_GROUPS = 64, 8, 8, 4
ROUTED_SCALE = 2.5
LN_EPS = 1e-5
RMS_EPS = 1e-6

LANES = 128
MLA_HEAD_PAD = 256
VMEM_LIMIT = 56 * 1024 * 1024
ROW_TILE = 512
MLA_Q_TILE = 1024
DIFF_Q_TILE = 512
ATTN_KV_CHUNK = 256
NA_PAIR = 2 * GRID_W
NA_KEYS = 10 * GRID_W
MOE_TILE = 256
ROW_GROUP = 8
TOKEN_BLOCK = LANES
SLOT_GROUP = 8
PREFIX_RADIX = 256.0
COMBINE_TILE = 128
ROUTER_TILE = 512


def _params(sem, vmem=VMEM_LIMIT):
    return pltpu.CompilerParams(dimension_semantics=sem, vmem_limit_bytes=vmem)


def _sigmoid(x):
    return 1.0 / (1.0 + jnp.exp(-x))


def _silu(x):
    return x * _sigmoid(x)


def _ln(x):
    mu = jnp.mean(x, axis=-1, keepdims=True)
    xc = x - mu
    var = jnp.mean(xc * xc, axis=-1, keepdims=True)
    return xc * lax.rsqrt(var + LN_EPS)


def _rms(x, g):
    return x * lax.rsqrt(jnp.mean(x * x, axis=-1, keepdims=True) + RMS_EPS) * g


def _bdot(a, b):
    return jnp.dot(a, b, preferred_element_type=F32)


def _split(a):
    hi = a.astype(BF16)
    lo = (a - hi.astype(F32)).astype(BF16)
    return hi, lo


def _dot3(a, b):
    ah, al = _split(a)
    bh, bl = _split(b)
    return _bdot(ah, bh) + (_bdot(al, bh) + _bdot(ah, bl))


def _rope(x, c, s1, s2, half):
    width = x.shape[1]
    reps = width // c.shape[1]
    if reps > 1:
        c, s1, s2 = (jnp.concatenate([t] * reps, axis=1) for t in (c, s1, s2))
    return x * c + pltpu.roll(x, width - half, 1) * s1 + pltpu.roll(x, half, 1) * s2


def _ada_kernel(c_ref, w_ref, b_ref, o_ref):
    o_ref[0] = _dot3(_silu(c_ref[...]), w_ref[0]) + b_ref[0]


def _ada(c, w_ada, b_ada):
    depth, d, d6 = w_ada.shape
    batch = c.shape[0]
    rows = 8
    cp = jnp.pad(c, ((0, rows - batch), (0, 0)))
    tn = 512
    out = pl.pallas_call(
        _ada_kernel,
        grid=(depth, d6 // tn),
        in_specs=[pl.BlockSpec((rows, d), lambda l, n: (0, 0)),
                  pl.BlockSpec((1, d, tn), lambda l, n: (l, 0, n)),
                  pl.BlockSpec((1, 1, tn), lambda l, n: (l, 0, n))],
        out_specs=pl.BlockSpec((1, rows, tn), lambda l, n: (l, 0, n)),
        out_shape=jax.ShapeDtypeStruct((depth, rows, d6), F32),
        compiler_params=_params(("arbitrary", "arbitrary")),
        name="ada",
    )(cp, w_ada, b_ada.reshape(depth, 1, d6))
    return out[:, :batch]


def _inproj_kernel(x_ref, sc_ref, sh_ref, wa_ref, wb_ref, wc_ref, c_ref, s1_ref, s2_ref,
                   ua_ref, ub_ref, qv_ref, kt_ref):
    nq = DIFF_HEADS * 2 * DIFF_QK
    h = (_ln(x_ref[0]) * (1.0 + sc_ref[0]) + sh_ref[0]).astype(BF16)
    ua_ref[0] = _bdot(h, wa_ref[...])
    ub_ref[0] = _bdot(h, wb_ref[...]).astype(BF16)
    acc = _bdot(h, wc_ref[...])
    r = _rope(acc[:, :2 * nq], c_ref[...], s1_ref[...], s2_ref[...], DIFF_ROT // 2)
    qv_ref[0, :, :nq] = (r[:, :nq] * (DIFF_QK ** -0.5)).astype(BF16)
    qv_ref[0, :, nq:] = acc[:, 2 * nq:].astype(BF16)
    for hd in range(DIFF_HEADS):
        kt_ref[0, hd] = r[:, nq + hd * DIFF_V:nq + (hd + 1) * DIFF_V].T.astype(BF16)


def _inproj(x, sc, sh, wa, wb, wc, tables):
    b, s, d = x.shape
    tm = min(ROW_TILE, s)
    n_qv = 2 * DIFF_HEADS * DIFF_V
    row = lambda w: pl.BlockSpec((1, tm, w), lambda bi, m: (bi, m, 0))
    per_b = pl.BlockSpec((1, 1, d), lambda bi, m: (bi, 0, 0))
    weight = lambda w: pl.BlockSpec(w.shape, lambda bi, m: (0, 0), pipeline_mode=pl.Buffered(1))
    return pl.pallas_call(
        _inproj_kernel,
        grid=(b, s // tm),
        in_specs=[row(d), per_b, per_b, weight(wa), weight(wb), weight(wc)]
                 + [pl.BlockSpec((tm, LANES), lambda bi, m: (m, 0))] * 3,
        out_specs=[row(wa.shape[1]), row(wb.shape[1]), row(n_qv),
                   pl.BlockSpec((1, DIFF_HEADS, DIFF_V, tm), lambda bi, m: (bi, 0, 0, m))],
        out_shape=[jax.ShapeDtypeStruct((b, s, wa.shape[1]), F32),
                   jax.ShapeDtypeStruct((b, s, wb.shape[1]), BF16),
                   jax.ShapeDtypeStruct((b, s, n_qv), BF16),
                   jax.ShapeDtypeStruct((b, DIFF_HEADS, DIFF_V, s), BF16)],
        compiler_params=_params(("arbitrary", "arbitrary")),
        name="inproj",
    )(x, sc, sh, wa, wb, wc, *tables)


def _mla_proj_kernel(ua_ref, qn_ref, kvn_ref, wq_ref, wkv_ref, cq_ref, sq1_ref, sq2_ref,
                     ck_ref, sk1_ref, sk2_ref, q_ref, kt_ref, v_ref):
    ua = ua_ref[0]
    kv_lo = MLA_Q_RANK + MLA_KV_RANK
    cq = _rms(ua[:, :MLA_Q_RANK], qn_ref[...])
    q = _bdot(cq.astype(BF16), wq_ref[...])
    q = _rope(q, cq_ref[...], sq1_ref[...], sq2_ref[...], MLA_ROPE // 2)
    q_ref[0] = (q * ((MLA_NOPE + MLA_ROPE) ** -0.5)).astype(BF16)
    ckv = _rms(ua[:, MLA_Q_RANK:kv_lo], kvn_ref[...])
    kv = _bdot(ckv.astype(BF16), wkv_ref[...])
    nk = MLA_HEADS * MLA_NOPE
    v_ref[0] = kv[:, nk:].astype(BF16)
    kr = _rope(ua[:, kv_lo:], ck_ref[...], sk1_ref[...], sk2_ref[...], MLA_ROPE // 2)
    krt = kr.T.astype(BF16)
    for h in range(MLA_HEADS):
        kt_ref[0, h, :MLA_NOPE, :] = kv[:, h * MLA_NOPE:(h + 1) * MLA_NOPE].T.astype(BF16)
        kt_ref[0, h, MLA_NOPE:, :] = krt


def _mla_proj(ua, q_norm, kv_norm, wq, wkv, tq, tk):
    b, s, wa = ua.shape
    tm = min(ROW_TILE, s)
    hq = MLA_HEADS * MLA_HEAD_PAD
    hv = MLA_HEADS * MLA_V
    const = lambda bi, m: (0, 0)
    tab = lambda w: pl.BlockSpec((tm, w), lambda bi, m: (m, 0))
    return pl.pallas_call(
        _mla_proj_kernel,
        grid=(b, s // tm),
        in_specs=[pl.BlockSpec((1, tm, wa), lambda bi, m: (bi, m, 0)),
                  pl.BlockSpec((1, MLA_Q_RANK), const),
                  pl.BlockSpec((1, MLA_KV_RANK), const),
                  pl.BlockSpec(wq.shape, const),
                  pl.BlockSpec(wkv.shape, const),
                  tab(MLA_HEAD_PAD), tab(MLA_HEAD_PAD), tab(MLA_HEAD_PAD),
                  tab(LANES), tab(LANES), tab(LANES)],
        out_specs=[pl.BlockSpec((1, tm, hq), lambda bi, m: (bi, m, 0)),
                   pl.BlockSpec((1, MLA_HEADS, MLA_HEAD_PAD, tm), lambda bi, m: (bi, 0, 0, m)),
                   pl.BlockSpec((1, tm, hv), lambda bi, m: (bi, m, 0))],
        out_shape=[jax.ShapeDtypeStruct((b, s, hq), BF16),
                   jax.ShapeDtypeStruct((b, MLA_HEADS, MLA_HEAD_PAD, s), BF16),
                   jax.ShapeDtypeStruct((b, s, hv), BF16)],
        compiler_params=_params(("arbitrary", "arbitrary")),
        name="mla_proj",
    )(ua, q_norm.reshape(1, -1), kv_norm.reshape(1, -1), wq, wkv, *tq, *tk)


class _Softmax:
    def __init__(self):
        self.m = self.acc = None

    def update(self, sc, v_ones):
        mc = jnp.max(sc, axis=1, keepdims=True)
        m_new = mc if self.m is None else jnp.maximum(self.m, mc)
        pv = _bdot(jnp.exp(sc - m_new).astype(BF16), v_ones)
        self.acc = pv if self.m is None else jnp.exp(self.m - m_new) * self.acc + pv
        self.m = m_new

    def result(self):
        dv = self.acc.shape[1] - LANES
        return self.acc[:, :dv] * (1.0 / self.acc[:, dv:dv + 1])


def _kv_chunks(s):
    kc = min(ATTN_KV_CHUNK, s)
    return [(c * kc, (c + 1) * kc) for c in range(s // kc)]


def _with_ones(v):
    return jnp.concatenate([v, jnp.ones((v.shape[0], LANES), v.dtype)], axis=1)


def _mla_attn_kernel(q_ref, kt_ref, v_ref, o_ref):
    q = q_ref[0]
    sm = _Softmax()
    for lo, hi in _kv_chunks(kt_ref.shape[3]):
        sm.update(_bdot(q, kt_ref[0, 0, :, lo:hi]), _with_ones(v_ref[0, lo:hi, :]))
    o_ref[0] = sm.result().astype(BF16)


def _mla_attn(q, kt, v):
    b, s, _ = q.shape
    tq = min(MLA_Q_TILE, s)
    return pl.pallas_call(
        _mla_attn_kernel,
        grid=(b, MLA_HEADS, s // tq),
        in_specs=[pl.BlockSpec((1, tq, MLA_HEAD_PAD), lambda bi, h, i: (bi, i, h)),
                  pl.BlockSpec((1, 1, MLA_HEAD_PAD, s), lambda bi, h, i: (bi, h, 0, 0)),
                  pl.BlockSpec((1, s, MLA_V), lambda bi, h, i: (bi, 0, h))],
        out_specs=pl.BlockSpec((1, tq, MLA_V), lambda bi, h, i: (bi, i, h)),
        out_shape=jax.ShapeDtypeStruct((b, s, MLA_HEADS * MLA_V), BF16),
        compiler_params=_params(("arbitrary", "arbitrary", "arbitrary")),
        name="mla_attn",
    )(q, kt, v)


def _diff_attn_kernel(q_ref, kt_ref, v_ref, lq1_ref, lk1_ref, lq2_ref, lk2_ref, sub_ref, o_ref, *, lam_init):
    q = q_ref[0]
    lane = lax.broadcasted_iota(jnp.int32, q.shape, 1)
    zero = jnp.zeros_like(q)
    q1 = jnp.where(lane < DIFF_QK, q, zero)
    q2 = jnp.where(lane >= DIFF_QK, q, zero)
    lam = (jnp.exp(jnp.sum(lq1_ref[...] * lk1_ref[...], axis=1, keepdims=True))
           - jnp.exp(jnp.sum(lq2_ref[...] * lk2_ref[...], axis=1, keepdims=True)) + lam_init)
    sm1, sm2 = _Softmax(), _Softmax()
    for lo, hi in _kv_chunks(kt_ref.shape[3]):
        kt = kt_ref[0, 0, :, lo:hi]
        v = _with_ones(v_ref[0, lo:hi, :])
        sm1.update(_bdot(q1, kt), v)
        sm2.update(_bdot(q2, kt), v)
    o = sm1.result() - lam * sm2.result()
    o_ref[0] = (_rms(o, sub_ref[...]) * (1.0 - lam_init)).astype(BF16)


def _diff_attn(uc, kt, lq1, lk1, lq2, lk2, subln, lam_init):
    b, s, _ = uc.shape
    tq = min(DIFF_Q_TILE, s)
    vec = lambda n: pl.BlockSpec((1, n), lambda bi, h, i: (0, 0))
    v_block0 = DIFF_HEADS
    return pl.pallas_call(
        functools.partial(_diff_attn_kernel, lam_init=lam_init),
        grid=(b, DIFF_HEADS, s // tq),
        in_specs=[pl.BlockSpec((1, tq, DIFF_V), lambda bi, h, i: (bi, i, h)),
                  pl.BlockSpec((1, 1, 2 * DIFF_QK, s), lambda bi, h, i: (bi, h, 0, 0)),
                  pl.BlockSpec((1, s, DIFF_V), lambda bi, h, i: (bi, 0, v_block0 + h)),
                  vec(DIFF_QK), vec(DIFF_QK), vec(DIFF_QK), vec(DIFF_QK), vec(DIFF_V)],
        out_specs=pl.BlockSpec((1, tq, DIFF_V), lambda bi, h, i: (bi, i, h)),
        out_shape=jax.ShapeDtypeStruct((b, s, DIFF_HEADS * DIFF_V), BF16),
        compiler_params=_params(("arbitrary", "arbitrary", "arbitrary")),
        name="diff_attn",
    )(uc, kt, uc, lq1.reshape(1, -1), lk1.reshape(1, -1), lq2.reshape(1, -1), lk2.reshape(1, -1),
      subln.reshape(1, -1))


def _na_plan(rows):
    kh = min(NA_WIN_H, rows)
    key_rows = NA_KEYS // GRID_W
    r0 = lambda r: min(max(r - kh // 2, 0), rows - kh)
    starts, cases, sigs = [], [], []
    for p in range(rows // 2):
        sr = min(r0(2 * p), rows - key_rows)
        sig = (r0(2 * p) - 2 * p, r0(2 * p + 1) - 2 * p - 1, sr - 2 * p)
        if sig not in sigs:
            sigs.append(sig)
        starts.append(sr)
        cases.append(sigs.index(sig))
    return np.array(starts, np.int32), np.array(cases, np.int32), sigs, kh


def _na_bias(rpb, rows):
    _, _, sigs, kh = _na_plan(rows)
    n_dr, n_dc = 2 * NA_WIN_H - 1, 2 * NA_WIN_W - 1
    key_rows = NA_KEYS // GRID_W
    col = np.arange(GRID_W)
    dc = np.clip(col[None, :] - col[:, None], -(NA_WIN_W - 1), NA_WIN_W - 1) + (NA_WIN_W - 1)
    pick_col = (dc[None] == np.arange(n_dc)[:, None, None]).astype(np.float32)
    c0 = np.clip(col - NA_WIN_W // 2, 0, GRID_W - NA_WIN_W)
    col_ok = (col[None, :] >= c0[:, None]) & (col[None, :] < c0[:, None] + NA_WIN_W)
    pick_row = np.zeros((len(sigs), 2, key_rows, n_dr), np.float32)
    for c, (d0, d1, dsr) in enumerate(sigs):
        for qr, r0_rel in enumerate((d0, d1 + 1)):
            for kr in range(key_rows):
                krel = dsr + kr
                if r0_rel <= krel < r0_rel + kh:
                    pick_row[c, qr, kr, krel - qr + NA_WIN_H - 1] = 1.0
    row_ok = pick_row.sum(-1) > 0
    exact = lax.Precision.HIGHEST
    toe = jnp.einsum("hdj,jqk->hdqk", rpb.astype(F32), pick_col, precision=exact)
    bias = jnp.einsum("cabd,hdqk->hcaqbk", pick_row, toe, precision=exact)
    ok = row_ok[:, :, None, :, None] & col_ok[None, None, :, None, :]
    bias = jnp.where(ok[None], bias, -jnp.inf)
    return bias.reshape(rpb.shape[0], len(sigs), NA_PAIR, NA_KEYS)


def _na_kernel(start_ref, case_ref, q_ref, k_ref, v_ref, bias_ref, o_ref):
    def body(p, carry):
        qs = pl.ds(pl.multiple_of(p * NA_PAIR, NA_PAIR), NA_PAIR)
        ks = pl.ds(pl.multiple_of(start_ref[p] * GRID_W, LANES), NA_KEYS)
        q = q_ref[0, qs, :]
        sc = lax.dot_general(q, k_ref[0, ks, :], (((1,), (1,)), ((), ())), preferred_element_type=F32)
        sc = sc * (NA_DIM ** -0.5) + bias_ref[0, case_ref[p]]
        e = jnp.exp(sc - jnp.max(sc, axis=1, keepdims=True))
        pr = e / jnp.sum(e, axis=1, keepdims=True)
        o_ref[0, qs, :] = _bdot(pr.astype(BF16), v_ref[0, ks, :]).astype(BF16)
        return carry

    lax.fori_loop(0, q_ref.shape[1] // NA_PAIR, body, 0, unroll=8)


def _na_attn(ub, bias):
    b, s, _ = ub.shape
    starts, cases, sigs, _ = _na_plan(s // GRID_W)
    blk = lambda off: pl.BlockSpec((1, s, NA_DIM), lambda bi, h, st, cs: (bi, 0, off + h))
    return pl.pallas_call(
        _na_kernel,
        grid_spec=pltpu.PrefetchScalarGridSpec(
            num_scalar_prefetch=2,
            grid=(b, NA_HEADS),
            in_specs=[blk(0), blk(NA_HEADS), blk(2 * NA_HEADS),
                      pl.BlockSpec((1, len(sigs), NA_PAIR, NA_KEYS), lambda bi, h, st, cs: (h, 0, 0, 0))],
            out_specs=pl.BlockSpec((1, s, NA_DIM), lambda bi, h, st, cs: (bi, 0, h)),
        ),
        out_shape=jax.ShapeDtypeStruct((b, s, NA_HEADS * NA_DIM), BF16),
        compiler_params=_params(("arbitrary", "arbitrary")),
        name="na_attn",
    )(jnp.asarray(starts), jnp.asarray(cases), ub, ub, ub, bias)


def _oproj_kernel(oa_ref, ob_ref, oc_ref, x_ref, ga_ref, wo_ref, g1_ref, b1_ref, scf_ref, shf_ref, rw_ref,
                  x1_ref, h2_ref, lg_ref, *, alpha):
    na, nb = oa_ref.shape[2], ob_ref.shape[2]
    acc = (_bdot(oa_ref[0], wo_ref[:na, :]) + _bdot(ob_ref[0], wo_ref[na:na + nb, :])
           + _bdot(oc_ref[0], wo_ref[na + nb:, :]))
    x1 = _ln(alpha * x_ref[0] + (1.0 + ga_ref[0]) * acc) * g1_ref[...] + b1_ref[...]
    x1_ref[0] = x1
    h2 = _ln(x1) * (1.0 + scf_ref[0]) + shf_ref[0]
    h2_ref[0] = h2
    lg_ref[0] = _dot3(h2, rw_ref[...])


def _oproj(oa, ob, oc, x, ga, wo, g1, b1, scf, shf, rw, alpha):
    b, s, d = x.shape
    tm = min(ROW_TILE // 2, s)
    row = lambda w: pl.BlockSpec((1, tm, w), lambda bi, m: (bi, m, 0))
    per_b = pl.BlockSpec((1, 1, d), lambda bi, m: (bi, 0, 0))
    const = lambda shape: pl.BlockSpec(shape, lambda bi, m: (0, 0))
    return pl.pallas_call(
        functools.partial(_oproj_kernel, alpha=alpha),
        grid=(b, s // tm),
        in_specs=[row(oa.shape[2]), row(ob.shape[2]), row(oc.shape[2]), row(d), per_b,
                  const(wo.shape), const((1, d)), const((1, d)), per_b, per_b, const(rw.shape)],
        out_specs=[row(d), row(d), row(LANES)],
        out_shape=[jax.ShapeDtypeStruct((b, s, d), F32), jax.ShapeDtypeStruct((b, s, d), F32),
                   jax.ShapeDtypeStruct((b, s, LANES), F32)],
        compiler_params=_params(("arbitrary", "arbitrary")),
        name="oproj",
    )(oa, ob, oc, x, ga, wo, g1.reshape(1, d), b1.reshape(1, d), scf, shf, rw)


def _first_max(cur, idx, axis, size):
    m = jnp.max(cur, axis=axis, keepdims=True)
    first = jnp.min(jnp.where(cur == m, idx, size), axis=axis, keepdims=True)
    return idx == first, m


def _router_kernel(lg_ref, rb_ref, wt_ref, lc_ref, ki_ref):
    tn = lg_ref.shape[0]
    gsz = N_EXPERTS // N_GROUPS
    neg = -jnp.inf
    scores = _sigmoid(lg_ref[...].T[:N_EXPERTS, :])
    biased = scores + rb_ref[...]
    b3 = biased.reshape(N_GROUPS, gsz, tn)
    i3 = lax.broadcasted_iota(jnp.int32, b3.shape, 1)
    pick1, m1 = _first_max(b3, i3, 1, gsz)
    m2 = jnp.max(jnp.where(pick1, neg, b3), axis=1, keepdims=True)
    gs = (m1 + m2).reshape(N_GROUPS, tn)
    gi = lax.broadcasted_iota(jnp.int32, gs.shape, 0)
    gsel = jnp.zeros(gs.shape, F32)
    for _ in range(TOPK_GROUPS):
        pick, _m = _first_max(gs, gi, 0, N_GROUPS)
        gsel = jnp.where(pick, 1.0, gsel)
        gs = jnp.where(pick, neg, gs)
    emask = jnp.broadcast_to(gsel.reshape(N_GROUPS, 1, tn), b3.shape).reshape(N_EXPERTS, tn)
    cur = jnp.where(emask > 0.0, biased, neg)
    ei = lax.broadcasted_iota(jnp.int32, cur.shape, 0)
    self_ = jnp.zeros(cur.shape, F32)
    for _ in range(TOP_K):
        pick, _m = _first_max(cur, ei, 0, N_EXPERTS)
        self_ = jnp.where(pick, 1.0, self_)
        cur = jnp.where(pick, neg, cur)
    sel = self_ > 0.0
    w = jnp.where(sel, scores, 0.0)
    gate = w / jnp.sum(w, axis=0, keepdims=True) * ROUTED_SCALE
    selb = self_.astype(BF16)
    lower = (lax.broadcasted_iota(jnp.int32, (N_EXPERTS, N_EXPERTS), 1)
             < lax.broadcasted_iota(jnp.int32, (N_EXPERTS, N_EXPERTS), 0)).astype(BF16)
    kidx = _bdot(lower, selb)
    ki_ref[...] = jnp.where(sel, kidx + 1.0, 0.0).astype(BF16)
    ti = lax.broadcasted_iota(jnp.int32, (tn, tn), 0)
    tj = lax.broadcasted_iota(jnp.int32, (tn, tn), 1)
    same_block = (ti // TOKEN_BLOCK) == (tj // TOKEN_BLOCK)
    lc_ref[...] = _bdot(selb, jnp.logical_and(ti <= tj, same_block).astype(BF16)).astype(BF16)
    w_rows = [jnp.sum(jnp.where(sel & (kidx == float(k)), gate, 0.0), axis=0, keepdims=True)
              for k in range(TOP_K)]
    wk = jnp.concatenate(w_rows + [jnp.zeros((LANES - TOP_K, tn), F32)], axis=0)
    wt_ref[...] = wk.T


def _router(logits, router_bias):
    n = logits.shape[0]
    tn = min(ROUTER_TILE, n)
    return pl.pallas_call(
        _router_kernel,
        grid=(n // tn,),
        in_specs=[pl.BlockSpec((tn, LANES), lambda t: (t, 0)),
                  pl.BlockSpec((N_EXPERTS, 1), lambda t: (0, 0))],
        out_specs=[pl.BlockSpec((tn, LANES), lambda t: (t, 0)),
                   pl.BlockSpec((N_EXPERTS, tn), lambda t: (0, t)),
                   pl.BlockSpec((N_EXPERTS, tn), lambda t: (0, t))],
        out_shape=[jax.ShapeDtypeStruct((n, LANES), F32),
                   jax.ShapeDtypeStruct((N_EXPERTS, n), BF16),
                   jax.ShapeDtypeStruct((N_EXPERTS, n), BF16)],
        compiler_params=_params(("arbitrary",)),
        name="router",
    )(logits, router_bias.reshape(N_EXPERTS, 1))


def _slots_kernel(te_ref, r0_ref, lc_ref, ki_ref, pin_ref, pexh_ref, pexl_ref, src_ref, dst_ref, *, n):
    g = pl.program_id(0)
    tm = src_ref.shape[2]
    nb = lc_ref.shape[1]
    slot_i = lax.broadcasted_iota(jnp.int32, (tm, 1), 0).astype(F32)
    blk_i = lax.broadcasted_iota(jnp.int32, (tm, nb), 1).astype(F32)
    ones_nb = jnp.ones((nb, LANES), BF16)
    ones_tb = jnp.ones((TOKEN_BLOCK, LANES), BF16)

    def lanes(x):
        return x.T[0:1, :].astype(jnp.int32)

    for j in range(SLOT_GROUP):
        i = g * SLOT_GROUP + j
        e = te_ref[i]
        r = slot_i + r0_ref[i].astype(F32)
        blk = _bdot((pin_ref[e] <= r).astype(BF16), ones_nb)
        ohb = (blk_i == blk[:, :nb]).astype(BF16)
        table = jnp.concatenate([jnp.broadcast_to(pexh_ref[e], (nb, LANES)).astype(BF16),
                                 jnp.broadcast_to(pexl_ref[e], (nb, LANES)).astype(BF16),
                                 lc_ref[e], ki_ref[e]], axis=1)
        row = _bdot(ohb, table)
        r_loc = r - (row[:, :LANES] * PREFIX_RADIX + row[:, LANES:2 * LANES])
        lc_row = row[:, 2 * LANES:2 * LANES + TOKEN_BLOCK]
        ki_row = row[:, 2 * LANES + TOKEN_BLOCK:]
        tl = _bdot((lc_row <= r_loc).astype(BF16), ones_tb)
        kk = _bdot(jnp.where(lc_row == r_loc + 1.0, ki_row, 0.0).astype(BF16), ones_tb) - 1.0
        tok = blk * TOKEN_BLOCK + tl
        valid = blk < nb
        src_ref[j] = lanes(jnp.where(valid, tok, 0.0))
        dst_ref[j] = lanes(jnp.where(valid, kk * n + tok, 0.0))


def _moe_plan(lc, ki, n):
    tm = MOE_TILE
    n_tiles = TOP_K * n // tm + N_EXPERTS
    nb = n // TOKEN_BLOCK
    lc3 = lc.reshape(N_EXPERTS, nb, TOKEN_BLOCK)
    cb = lc3[:, :, TOKEN_BLOCK - 1].astype(F32)
    pin = jnp.cumsum(cb, axis=1)
    pex = pin - cb
    pex_hi = jnp.floor(pex / PREFIX_RADIX)
    counts = pin[:, nb - 1].astype(jnp.int32)
    ntile = (counts + tm - 1) // tm
    tile_end = jnp.cumsum(ntile)
    tile_start = tile_end - ntile
    n_used = tile_end[-1]
    ti = jnp.minimum(jnp.arange(n_tiles, dtype=jnp.int32), n_used - 1)
    te = jnp.minimum(jnp.sum(tile_end[None, :] <= ti[:, None], axis=1), N_EXPERTS - 1).astype(jnp.int32)
    onehot_te = te[:, None] == jnp.arange(N_EXPERTS, dtype=jnp.int32)[None, :]
    start_te = jnp.sum(jnp.where(onehot_te, tile_start[None, :], 0), axis=1)
    count_te = jnp.sum(jnp.where(onehot_te, counts[None, :], 0), axis=1)
    first = (jnp.arange(n_tiles, dtype=jnp.int32) == start_te).astype(jnp.int32)
    r0 = ((ti - start_te) * tm).astype(jnp.int32)
    nvalid = jnp.clip(count_te - r0, 0, tm).astype(jnp.int32)
    ar = jnp.arange(N_EXPERTS, dtype=jnp.int32)
    later_used = jnp.logical_and(ar[None, :] > ar[:, None], (counts > 0)[None, :])
    next_used = jnp.min(jnp.where(later_used, ar[None, :], N_EXPERTS), axis=1)
    nxt_e = jnp.sum(jnp.where(onehot_te, next_used[None, :], 0), axis=1).astype(jnp.int32)
    whole = lambda shape: pl.BlockSpec(shape, lambda g, *_: (0,) * len(shape))
    out_blk = pl.BlockSpec((SLOT_GROUP, 1, tm), lambda g, *_: (g, 0, 0))
    src, dst = pl.pallas_call(
        functools.partial(_slots_kernel, n=n),
        grid_spec=pltpu.PrefetchScalarGridSpec(
            num_scalar_prefetch=2,
            grid=(n_tiles // SLOT_GROUP,),
            in_specs=[whole((N_EXPERTS, nb, TOKEN_BLOCK)), whole((N_EXPERTS, nb, TOKEN_BLOCK)),
                      whole((N_EXPERTS, 1, nb)), whole((N_EXPERTS, nb, 1)), whole((N_EXPERTS, nb, 1))],
            out_specs=[out_blk, out_blk],
        ),
        out_shape=[jax.ShapeDtypeStruct((n_tiles, 1, tm), jnp.int32)] * 2,
        compiler_params=_params(("arbitrary",)),
        name="moe_slots",
    )(te, r0, lc3, ki.reshape(N_EXPERTS, nb, TOKEN_BLOCK), pin.reshape(N_EXPERTS, 1, nb),
      pex_hi.reshape(N_EXPERTS, nb, 1), (pex - pex_hi * PREFIX_RADIX).reshape(N_EXPERTS, nb, 1))
    return te, first, nvalid, nxt_e, n_used.reshape(1).astype(jnp.int32), src, dst


def _moe_kernel(te_ref, first_ref, nvalid_ref, nxte_ref, nused_ref, src_cur, src_nxt, dst_cur, dst_prv, h_hbm,
                wg_hbm, wu_hbm, wd_hbm, y_hbm, xbuf0, xbuf1, ybuf0, ybuf1, wgf, wuf, wdf, wgb, wub, wdb,
                gsem, ssem, wsem, *, layer):
    i = pl.program_id(0)
    tm = xbuf0.shape[0]
    n_used = nused_ref[0]
    nv = nvalid_ref[i]
    xbufs, ybufs = (xbuf0, xbuf1), (ybuf0, ybuf1)

    def gather(src_ref, p):
        for r in range(tm):
            pltpu.make_async_copy(h_hbm.at[pl.ds(src_ref[0, 0, r], 1)], xbufs[p].at[pl.ds(r, 1)],
                                  gsem.at[p]).start()

    def gather_wait(p):
        pltpu.make_async_copy(h_hbm.at[pl.ds(0, tm)], xbufs[p], gsem.at[p]).wait()

    def scatter_row(dst_ref, p, r):
        pltpu.make_async_copy(ybufs[p].at[pl.ds(r, 1)], y_hbm.at[pl.ds(dst_ref[0, 0, r], 1)],
                              ssem.at[p]).start()

    def scatter_wait(p, rows):
        @pl.when(rows == tm)
        def _():
            pltpu.make_async_copy(ybufs[p], y_hbm.at[pl.ds(0, tm)], ssem.at[p]).wait()

        @pl.when(rows < tm)
        def _():
            def wait_rows(count):
                def body(r, carry):
                    pltpu.make_async_copy(ybufs[p].at[pl.ds(0, count)], y_hbm.at[pl.ds(0, count)],
                                          ssem.at[p]).wait()
                    return carry
                return body

            lax.fori_loop(0, rows // ROW_GROUP, wait_rows(ROW_GROUP), 0)
            lax.fori_loop(0, rows % ROW_GROUP, wait_rows(1), 0)

    def weight_copies(e):
        return [pltpu.make_async_copy(hbm.at[layer, e], buf, wsem.at[k])
                for k, (hbm, buf) in enumerate(((wg_hbm, wgf), (wu_hbm, wuf), (wd_hbm, wdf)))]

    def step(p):
        @pl.when(i >= 2)
        def _():
            scatter_wait(p, nvalid_ref[i - 2])

        gather_wait(p)

        @pl.when(first_ref[i] == 1)
        def _():
            @pl.when(i == 0)
            def _():
                for cp in weight_copies(te_ref[0]):
                    cp.start(priority=1)

            for cp in weight_copies(te_ref[i]):
                cp.wait()
            wgb[...] = wgf[...].astype(BF16)
            wub[...] = wuf[...].astype(BF16)
            wdb[...] = wdf[...].astype(BF16)
            nxt = nxte_ref[i]

            @pl.when(nxt < N_EXPERTS)
            def _():
                for cp in weight_copies(nxt):
                    cp.start(priority=1)

        def main(scatter_prev):
            gather(src_nxt, 1 - p)
            if scatter_prev:
                for r in range(tm):
                    scatter_row(dst_prv, 1 - p, r)
            x = xbufs[p][...].astype(BF16)
            act = _silu(_bdot(x, wgb[...])) * _bdot(x, wub[...])
            ybufs[p][...] = _bdot(act.astype(BF16), wdb[...])

        prev_full = jnp.logical_and(i >= 1, nvalid_ref[jnp.maximum(i - 1, 0)] == tm)

        @pl.when(prev_full)
        def _():
            main(True)

        @pl.when(jnp.logical_not(prev_full))
        def _():
            main(False)

        last = i == n_used - 1

        @pl.when(jnp.logical_or(nv < tm, last))
        def _():
            def group_body(g, carry):
                for u in range(ROW_GROUP):
                    scatter_row(dst_cur, p, g * ROW_GROUP + u)
                return carry

            def row_body(r, carry):
                scatter_row(dst_cur, p, r)
                return carry

            groups = nv // ROW_GROUP
            lax.fori_loop(0, groups, group_body, 0)
            lax.fori_loop(groups * ROW_GROUP, nv, row_body, 0)

        @pl.when(last)
        def _():
            gather_wait(1 - p)
            scatter_wait(p, nv)

            @pl.when(i >= 1)
            def _():
                scatter_wait(1 - p, nvalid_ref[i - 1])

    @pl.when(i < n_used)
    def _():
        @pl.when(i == 0)
        def _():
            gather(src_cur, 0)

        for p in range(2):
            @pl.when(i % 2 == p)
            def _():
                step(p)


def _moe(h2, plan, w_gate, w_up, w_down, layer):
    n, d = h2.shape
    te, first, nvalid, nxt_e, n_used, src, dst = plan
    n_tiles, _, tm = src.shape
    f = w_gate.shape[3]
    idx_blk = lambda fn: pl.BlockSpec((1, 1, tm), fn, memory_space=pltpu.SMEM)
    hbm = pl.BlockSpec(memory_space=pl.ANY)
    return pl.pallas_call(
        functools.partial(_moe_kernel, layer=layer),
        grid_spec=pltpu.PrefetchScalarGridSpec(
            num_scalar_prefetch=5,
            grid=(n_tiles,),
            in_specs=[idx_blk(lambda i, *_: (i, 0, 0)),
                      idx_blk(lambda i, *_: (jnp.minimum(i + 1, n_tiles - 1), 0, 0)),
                      idx_blk(lambda i, *_: (i, 0, 0)),
                      idx_blk(lambda i, *_: (jnp.maximum(i - 1, 0), 0, 0)),
                      hbm, hbm, hbm, hbm],
            out_specs=hbm,
            scratch_shapes=[pltpu.VMEM((tm, d), F32)] * 4
                           + [pltpu.VMEM((d, f), F32), pltpu.VMEM((d, f), F32), pltpu.VMEM((f, d), F32),
                              pltpu.VMEM((d, f), BF16), pltpu.VMEM((d, f), BF16), pltpu.VMEM((f, d), BF16),
                              pltpu.SemaphoreType.DMA((2,)), pltpu.SemaphoreType.DMA((2,)),
                              pltpu.SemaphoreType.DMA((3,))],
        ),
        out_shape=jax.ShapeDtypeStruct((TOP_K * n, d), F32),
        compiler_params=_params(("arbitrary",)),
        name="moe_experts",
    )(te, first, nvalid, nxt_e, n_used, src, src, dst, dst, h2, w_gate, w_up, w_down)


def _combine_kernel(*refs, alpha):
    y_refs = refs[:TOP_K]
    wt_ref, h_ref, x_ref, gf_ref, sg_ref, su_ref, sd_ref, g2_ref, b2_ref, o_ref = refs[TOP_K:]
    wt = wt_ref[...]
    routed = wt[:, 0:1] * y_refs[0][...]
    for k in range(1, TOP_K):
        routed = routed + wt[:, k:k + 1] * y_refs[k][...]
    hb = h_ref[...].astype(BF16)
    act = _silu(_bdot(hb, sg_ref[...])) * _bdot(hb, su_ref[...])
    y = routed + _bdot(act.astype(BF16), sd_ref[...])
    o_ref[...] = _ln(alpha * x_ref[...] + (1.0 + gf_ref[0]) * y) * g2_ref[...] + b2_ref[...]


def _combine(y_tok, wt, h2, x1, gf, sg, su, sd, g2, b2, alpha, seq):
    n, d = h2.shape
    tm = min(COMBINE_TILE, seq)
    nblk = n // tm
    row = pl.BlockSpec((tm, d), lambda i: (i, 0))
    const = lambda shape: pl.BlockSpec(shape, lambda i: (0, 0))
    y_specs = [pl.BlockSpec((tm, d), (lambda i, k=k: (k * nblk + i, 0))) for k in range(TOP_K)]
    return pl.pallas_call(
        functools.partial(_combine_kernel, alpha=alpha),
        grid=(nblk,),
        in_specs=y_specs + [pl.BlockSpec((tm, LANES), lambda i: (i, 0)), row, row,
                            pl.BlockSpec((1, 1, d), lambda i: (i // (seq // tm), 0, 0)),
                            const(sg.shape), const(su.shape), const(sd.shape), const((1, d)), const((1, d))],
        out_specs=row,
        out_shape=jax.ShapeDtypeStruct((n, d), F32),
        compiler_params=_params(("arbitrary",)),
        name="combine",
    )(*([y_tok] * TOP_K), wt, h2, x1, gf, sg, su, sd, g2.reshape(1, d), b2.reshape(1, d))


def _rope_tables(s, half, width, offset, period=None):
    inv = ROPE_THETA ** (-jnp.arange(half, dtype=F32) / half)
    ang = jnp.arange(s, dtype=jnp.int32).astype(F32)[:, None] * inv[None, :]
    cos, sin = jnp.cos(ang), jnp.sin(ang)
    period = period or width
    tail = period - offset - 2 * half
    one = lambda n: jnp.ones((s, n), F32)
    zero = lambda n: jnp.zeros((s, n), F32)
    c = jnp.concatenate([one(offset), cos, cos, one(tail)], axis=1)
    s1 = jnp.concatenate([zero(offset), -sin, zero(half + tail)], axis=1)
    s2 = jnp.concatenate([zero(offset + half), sin, zero(tail)], axis=1)
    reps = width // period
    return tuple(jnp.tile(t, (1, reps)) for t in (c, s1, s2))


def kernel(x, c, w_ada, b_ada, w_in, mla_q_norm, mla_w_uq, mla_kv_norm, mla_w_ukv, na_rpb, diff_lq1, diff_lk1,
           diff_lq2, diff_lk2, diff_subln, w_o, ln1_g, ln1_b, router_w, router_bias, exp_w_gate, exp_w_up,
           exp_w_down, sh_w_gate, sh_w_up, sh_w_down, ln2_g, ln2_b):
    b, s, d = x.shape
    depth = w_ada.shape[0]
    n = b * s
    alpha = (2 * depth) ** 0.25
    ada = _ada(c, w_ada, b_ada)
    tab_diff = _rope_tables(s, DIFF_ROT // 2, LANES, 0, period=DIFF_QK)
    tab_q = _rope_tables(s, MLA_ROPE // 2, MLA_HEAD_PAD, MLA_NOPE)
    tab_k = _rope_tables(s, MLA_ROPE // 2, LANES, 0)
    a_pad = LANES - MLA_ROPE

    for i in range(depth):
        lam_init = 0.8 - 0.6 * math.exp(-0.3 * i)
        sh_a, sc_a, g_a, sh_f, sc_f, g_f = [a[:, None, :] for a in jnp.split(ada[i], 6, -1)]
        wi = w_in[i]
        w_a = jnp.pad(wi[:, :IN_A], ((0, 0), (0, a_pad))).astype(BF16)
        w_b = wi[:, IN_A:IN_A + IN_B].astype(BF16)
        w_c = wi[:, IN_A + IN_B:].astype(BF16)
        wq = jnp.pad(mla_w_uq[i].reshape(MLA_Q_RANK, MLA_HEADS, MLA_NOPE + MLA_ROPE),
                     ((0, 0), (0, 0), (0, MLA_HEAD_PAD - MLA_NOPE - MLA_ROPE)))
        wq = wq.reshape(MLA_Q_RANK, MLA_HEADS * MLA_HEAD_PAD).astype(BF16)
        wkv3 = mla_w_ukv[i].reshape(MLA_KV_RANK, MLA_HEADS, MLA_NOPE + MLA_V)
        wkv = jnp.concatenate([wkv3[:, :, :MLA_NOPE].reshape(MLA_KV_RANK, -1),
                               wkv3[:, :, MLA_NOPE:].reshape(MLA_KV_RANK, -1)], axis=1).astype(BF16)
        rw = jnp.pad(router_w[i], ((0, 0), (0, LANES - N_EXPERTS)))

        ua, ub, uc, kt_c = _inproj(x, sc_a, sh_a, w_a, w_b, w_c, tab_diff)
        q_a, kt_a, v_a = _mla_proj(ua, mla_q_norm[i], mla_kv_norm[i], wq, wkv, tab_q, tab_k)
        o_a = _mla_attn(q_a, kt_a, v_a)
        o_b = _na_attn(ub, _na_bias(na_rpb[i], s // GRID_W))
        o_c = _diff_attn(uc, kt_c, diff_lq1[i], diff_lk1[i], diff_lq2[i], diff_lk2[i], diff_subln[i], lam_init)
        x1, h2, logits = _oproj(o_a, o_b, o_c, x, g_a, w_o[i].astype(BF16), ln1_g[i], ln1_b[i], sc_f, sh_f,
                                rw, alpha)

        h2 = h2.reshape(n, d)
        wt, lc, ki = _router(logits.reshape(n, LANES), router_bias[i])
        plan = _moe_plan(lc, ki, n)
        y_tok = _moe(h2, plan, exp_w_gate, exp_w_up, exp_w_down, i)
        x = _combine(y_tok, wt, h2, x1.reshape(n, d), g_f, sh_w_gate[i].astype(BF16), sh_w_up[i].astype(BF16),
                     sh_w_down[i].astype(BF16), ln2_g[i], ln2_b[i], alpha, s).reshape(b, s, d)
    return x
```

```python
import functools
import math

import numpy as np
import jax
import jax.numpy as jnp
from jax import lax
from jax.experimental import pallas as pl
from jax.experimental.pallas import tpu as pltpu

F32 = jnp.float32
BF16 = jnp.bfloat16

GRID_W = 64
ROPE_THETA = 500000.0
MLA_HEADS, MLA_Q_RANK, MLA_KV_RANK, MLA_NOPE, MLA_ROPE, MLA_V = 8, 512, 256, 128, 64, 128
NA_HEADS, NA_DIM, NA_WIN_H, NA_WIN_W = 4, 128, 8, 16
DIFF_HEADS, DIFF_QK = 4, 64
DIFF_V = 2 * DIFF_QK
DIFF_ROT = DIFF_QK // 4
IN_A = MLA_Q_RANK + MLA_KV_RANK + MLA_ROPE
IN_B = 3 * NA_HEADS * NA_DIM
N_EXPERTS, TOP_K, N_GROUPS, TOPK_GROUPS = 64, 8, 8, 4
ROUTED_SCALE = 2.5
LN_EPS = 1e-5
RMS_EPS = 1e-6

LANES = 128
MLA_HEAD_PAD = 256
VMEM_LIMIT = 56 * 1024 * 1024
ROW_TILE = 512
MLA_Q_TILE = 1024
DIFF_Q_TILE = 512
ATTN_KV_CHUNK = 256
NA_PAIR = 2 * GRID_W
NA_KEYS = 10 * GRID_W
MOE_TILE = 256
ROW_GROUP = 8
TOKEN_BLOCK = LANES
SLOT_GROUP = 8
PREFIX_RADIX = 256.0
COMBINE_TILE = 128
Y_TAIL_TILES = 8
ROUTER_TILE = 512


def _params(sem, vmem=VMEM_LIMIT):
    return pltpu.CompilerParams(dimension_semantics=sem, vmem_limit_bytes=vmem)


def _sigmoid(x):
    return 1.0 / (1.0 + jnp.exp(-x))


def _silu(x):
    return x * _sigmoid(x)


def _ln(x):
    mu = jnp.mean(x, axis=-1, keepdims=True)
    xc = x - mu
    var = jnp.mean(xc * xc, axis=-1, keepdims=True)
    return xc * lax.rsqrt(var + LN_EPS)


def _rms(x, g):
    return x * lax.rsqrt(jnp.mean(x * x, axis=-1, keepdims=True) + RMS_EPS) * g


def _bdot(a, b):
    return jnp.dot(a, b, preferred_element_type=F32)


def _split(a):
    hi = a.astype(BF16)
    lo = (a - hi.astype(F32)).astype(BF16)
    return hi, lo


def _dot3(a, b):
    ah, al = _split(a)
    bh, bl = _split(b)
    return _bdot(ah, bh) + (_bdot(al, bh) + _bdot(ah, bl))


def _rope(x, c, s1, s2, half):
    width = x.shape[1]
    reps = width // c.shape[1]
    if reps > 1:
        c, s1, s2 = (jnp.concatenate([t] * reps, axis=1) for t in (c, s1, s2))
    return x * c + pltpu.roll(x, width - half, 1) * s1 + pltpu.roll(x, half, 1) * s2


def _ada_kernel(c_ref, w_ref, b_ref, o_ref):
    o_ref[0] = _dot3(_silu(c_ref[...]), w_ref[0]) + b_ref[0]


def _ada(c, w_ada, b_ada):
    depth, d, d6 = w_ada.shape
    batch = c.shape[0]
    rows = 8
    cp = jnp.pad(c, ((0, rows - batch), (0, 0)))
    tn = 512
    out = pl.pallas_call(
        _ada_kernel,
        grid=(depth, d6 // tn),
        in_specs=[pl.BlockSpec((rows, d), lambda l, n: (0, 0)),
                  pl.BlockSpec((1, d, tn), lambda l, n: (l, 0, n)),
                  pl.BlockSpec((1, 1, tn), lambda l, n: (l, 0, n))],
        out_specs=pl.BlockSpec((1, rows, tn), lambda l, n: (l, 0, n)),
        out_shape=jax.ShapeDtypeStruct((depth, rows, d6), F32),
        compiler_params=_params(("arbitrary", "arbitrary")),
        name="ada",
    )(cp, w_ada, b_ada.reshape(depth, 1, d6))
    return out[:, :batch]


def _inproj_kernel(x_ref, sc_ref, sh_ref, wa_ref, wb_ref, wc_ref, c_ref, s1_ref, s2_ref,
                   ua_ref, ub_ref, qv_ref, kt_ref):
    nq = DIFF_HEADS * 2 * DIFF_QK
    h = (_ln(x_ref[0]) * (1.0 + sc_ref[0]) + sh_ref[0]).astype(BF16)
    ua_ref[0] = _bdot(h, wa_ref[...])
    ub_ref[0] = _bdot(h, wb_ref[...]).astype(BF16)
    acc = _bdot(h, wc_ref[...])
    r = _rope(acc[:, :2 * nq], c_ref[...], s1_ref[...], s2_ref[...], DIFF_ROT // 2)
    qv_ref[0, :, :nq] = (r[:, :nq] * (DIFF_QK ** -0.5)).astype(BF16)
    qv_ref[0, :, nq:] = acc[:, 2 * nq:].astype(BF16)
    for hd in range(DIFF_HEADS):
        kt_ref[0, hd] = r[:, nq + hd * DIFF_V:nq + (hd + 1) * DIFF_V].T.astype(BF16)


def _inproj(x, sc, sh, wa, wb, wc, tables):
    b, s, d = x.shape
    tm = min(ROW_TILE, s)
    n_qv = 2 * DIFF_HEADS * DIFF_V
    row = lambda w: pl.BlockSpec((1, tm, w), lambda bi, m: (bi, m, 0))
    per_b = pl.BlockSpec((1, 1, d), lambda bi, m: (bi, 0, 0))
    weight = lambda w: pl.BlockSpec(w.shape, lambda bi, m: (0, 0), pipeline_mode=pl.Buffered(1))
    return pl.pallas_call(
        _inproj_kernel,
        grid=(b, s // tm),
        in_specs=[row(d), per_b, per_b, weight(wa), weight(wb), weight(wc)]
                 + [pl.BlockSpec((tm, LANES), lambda bi, m: (m, 0))] * 3,
        out_specs=[row(wa.shape[1]), row(wb.shape[1]), row(n_qv),
                   pl.BlockSpec((1, DIFF_HEADS, DIFF_V, tm), lambda bi, m: (bi, 0, 0, m))],
        out_shape=[jax.ShapeDtypeStruct((b, s, wa.shape[1]), F32),
                   jax.ShapeDtypeStruct((b, s, wb.shape[1]), BF16),
                   jax.ShapeDtypeStruct((b, s, n_qv), BF16),
                   jax.ShapeDtypeStruct((b, DIFF_HEADS, DIFF_V, s), BF16)],
        compiler_params=_params(("arbitrary", "arbitrary")),
        name="inproj",
    )(x, sc, sh, wa, wb, wc, *tables)


def _mla_proj_kernel(ua_ref, qn_ref, kvn_ref, wq_ref, wkv_ref, cq_ref, sq1_ref, sq2_ref,
                     ck_ref, sk1_ref, sk2_ref, q_ref, kt_ref, v_ref):
    ua = ua_ref[0]
    kv_lo = MLA_Q_RANK + MLA_KV_RANK
    cq = _rms(ua[:, :MLA_Q_RANK], qn_ref[...])
    q = _bdot(cq.astype(BF16), wq_ref[...])
    q = _rope(q, cq_ref[...], sq1_ref[...], sq2_ref[...], MLA_ROPE // 2)
    q_ref[0] = (q * ((MLA_NOPE + MLA_ROPE) ** -0.5)).astype(BF16)
    ckv = _rms(ua[:, MLA_Q_RANK:kv_lo], kvn_ref[...])
    kv = _bdot(ckv.astype(BF16), wkv_ref[...])
    nk = MLA_HEADS * MLA_NOPE
    v_ref[0] = kv[:, nk:].astype(BF16)
    kr = _rope(ua[:, kv_lo:], ck_ref[...], sk1_ref[...], sk2_ref[...], MLA_ROPE // 2)
    krt = kr.T.astype(BF16)
    for h in range(MLA_HEADS):
        kt_ref[0, h, :MLA_NOPE, :] = kv[:, h * MLA_NOPE:(h + 1) * MLA_NOPE].T.astype(BF16)
        kt_ref[0, h, MLA_NOPE:, :] = krt


def _mla_proj(ua, q_norm, kv_norm, wq, wkv, tq, tk):
    b, s, wa = ua.shape
    tm = min(ROW_TILE, s)
    hq = MLA_HEADS * MLA_HEAD_PAD
    hv = MLA_HEADS * MLA_V
    const = lambda bi, m: (0, 0)
    tab = lambda w: pl.BlockSpec((tm, w), lambda bi, m: (m, 0))
    return pl.pallas_call(
        _mla_proj_kernel,
        grid=(b, s // tm),
        in_specs=[pl.BlockSpec((1, tm, wa), lambda bi, m: (bi, m, 0)),
                  pl.BlockSpec((1, MLA_Q_RANK), const),
                  pl.BlockSpec((1, MLA_KV_RANK), const),
                  pl.BlockSpec(wq.shape, const),
                  pl.BlockSpec(wkv.shape, const),
                  tab(MLA_HEAD_PAD), tab(MLA_HEAD_PAD), tab(MLA_HEAD_PAD),
                  tab(LANES), tab(LANES), tab(LANES)],
        out_specs=[pl.BlockSpec((1, tm, hq), lambda bi, m: (bi, m, 0)),
                   pl.BlockSpec((1, MLA_HEADS, MLA_HEAD_PAD, tm), lambda bi, m: (bi, 0, 0, m)),
                   pl.BlockSpec((1, tm, hv), lambda bi, m: (bi, m, 0))],
        out_shape=[jax.ShapeDtypeStruct((b, s, hq), BF16),
                   jax.ShapeDtypeStruct((b, MLA_HEADS, MLA_HEAD_PAD, s), BF16),
                   jax.ShapeDtypeStruct((b, s, hv), BF16)],
        compiler_params=_params(("arbitrary", "arbitrary")),
        name="mla_proj",
    )(ua, q_norm.reshape(1, -1), kv_norm.reshape(1, -1), wq, wkv, *tq, *tk)


class _Softmax:
    def __init__(self):
        self.m = self.acc = None

    def update(self, sc, v_ones):
        mc = jnp.max(sc, axis=1, keepdims=True)
        m_new = mc if self.m is None else jnp.maximum(self.m, mc)
        pv = _bdot(jnp.exp(sc - m_new).astype(BF16), v_ones)
        self.acc = pv if self.m is None else jnp.exp(self.m - m_new) * self.acc + pv
        self.m = m_new

    def result(self):
        dv = self.acc.shape[1] - LANES
        return self.acc[:, :dv] * (1.0 / self.acc[:, dv:dv + 1])


def _kv_chunks(s):
    kc = min(ATTN_KV_CHUNK, s)
    return [(c * kc, (c + 1) * kc) for c in range(s // kc)]


def _with_ones(v):
    return jnp.concatenate([v, jnp.ones((v.shape[0], LANES), v.dtype)], axis=1)


def _mla_attn_kernel(q_ref, kt_ref, v_ref, o_ref):
    q = q_ref[0]
    sm = _Softmax()
    for lo, hi in _kv_chunks(kt_ref.shape[3]):
        sm.update(_bdot(q, kt_ref[0, 0, :, lo:hi]), _with_ones(v_ref[0, lo:hi, :]))
    o_ref[0] = sm.result().astype(BF16)


def _mla_attn(q, kt, v):
    b, s, _ = q.shape
    tq = min(MLA_Q_TILE, s)
    return pl.pallas_call(
        _mla_attn_kernel,
        grid=(b, MLA_HEADS, s // tq),
        in_specs=[pl.BlockSpec((1, tq, MLA_HEAD_PAD), lambda bi, h, i: (bi, i, h)),
                  pl.BlockSpec((1, 1, MLA_HEAD_PAD, s), lambda bi, h, i: (bi, h, 0, 0)),
                  pl.BlockSpec((1, s, MLA_V), lambda bi, h, i: (bi, 0, h))],
        out_specs=pl.BlockSpec((1, tq, MLA_V), lambda bi, h, i: (bi, i, h)),
        out_shape=jax.ShapeDtypeStruct((b, s, MLA_HEADS * MLA_V), BF16),
        compiler_params=_params(("arbitrary", "arbitrary", "arbitrary")),
        name="mla_attn",
    )(q, kt, v)


def _diff_attn_kernel(q_ref, kt_ref, v_ref, lq1_ref, lk1_ref, lq2_ref, lk2_ref, sub_ref, o_ref, *, lam_init):
    q = q_ref[0]
    lane = lax.broadcasted_iota(jnp.int32, q.shape, 1)
    zero = jnp.zeros_like(q)
    q1 = jnp.where(lane < DIFF_QK, q, zero)
    q2 = jnp.where(lane >= DIFF_QK, q, zero)
    lam = (jnp.exp(jnp.sum(lq1_ref[...] * lk1_ref[...], axis=1, keepdims=True))
           - jnp.exp(jnp.sum(lq2_ref[...] * lk2_ref[...], axis=1, keepdims=True)) + lam_init)
    sm1, sm2 = _Softmax(), _Softmax()
    for lo, hi in _kv_chunks(kt_ref.shape[3]):
        kt = kt_ref[0, 0, :, lo:hi]
        v = _with_ones(v_ref[0, lo:hi, :])
        sm1.update(_bdot(q1, kt), v)
        sm2.update(_bdot(q2, kt), v)
    o = sm1.result() - lam * sm2.result()
    o_ref[0] = (_rms(o, sub_ref[...]) * (1.0 - lam_init)).astype(BF16)


def _diff_attn(uc, kt, lq1, lk1, lq2, lk2, subln, lam_init):
    b, s, _ = uc.shape
    tq = min(DIFF_Q_TILE, s)
    vec = lambda n: pl.BlockSpec((1, n), lambda bi, h, i: (0, 0))
    v_block0 = DIFF_HEADS
    return pl.pallas_call(
        functools.partial(_diff_attn_kernel, lam_init=lam_init),
        grid=(b, DIFF_HEADS, s // tq),
        in_specs=[pl.BlockSpec((1, tq, DIFF_V), lambda bi, h, i: (bi, i, h)),
                  pl.BlockSpec((1, 1, 2 * DIFF_QK, s), lambda bi, h, i: (bi, h, 0, 0)),
                  pl.BlockSpec((1, s, DIFF_V), lambda bi, h, i: (bi, 0, v_block0 + h)),
                  vec(DIFF_QK), vec(DIFF_QK), vec(DIFF_QK), vec(DIFF_QK), vec(DIFF_V)],
        out_specs=pl.BlockSpec((1, tq, DIFF_V), lambda bi, h, i: (bi, i, h)),
        out_shape=jax.ShapeDtypeStruct((b, s, DIFF_HEADS * DIFF_V), BF16),
        compiler_params=_params(("arbitrary", "arbitrary", "arbitrary")),
        name="diff_attn",
    )(uc, kt, uc, lq1.reshape(1, -1), lk1.reshape(1, -1), lq2.reshape(1, -1), lk2.reshape(1, -1),
      subln.reshape(1, -1))


def _na_plan(rows):
    kh = min(NA_WIN_H, rows)
    key_rows = NA_KEYS // GRID_W
    r0 = lambda r: min(max(r - kh // 2, 0), rows - kh)
    starts, cases, sigs = [], [], []
    for p in range(rows // 2):
        sr = min(r0(2 * p), rows - key_rows)
        sig = (r0(2 * p) - 2 * p, r0(2 * p + 1) - 2 * p - 1, sr - 2 * p)
        if sig not in sigs:
            sigs.append(sig)
        starts.append(sr)
        cases.append(sigs.index(sig))
    return np.array(starts, np.int32), np.array(cases, np.int32), sigs, kh


def _na_bias(rpb, rows):
    _, _, sigs, kh = _na_plan(rows)
    n_dr, n_dc = 2 * NA_WIN_H - 1, 2 * NA_WIN_W - 1
    key_rows = NA_KEYS // GRID_W
    col = np.arange(GRID_W)
    dc = np.clip(col[None, :] - col[:, None], -(NA_WIN_W - 1), NA_WIN_W - 1) + (NA_WIN_W - 1)
    pick_col = (dc[None] == np.arange(n_dc)[:, None, None]).astype(np.float32)
    c0 = np.clip(col - NA_WIN_W // 2, 0, GRID_W - NA_WIN_W)
    col_ok = (col[None, :] >= c0[:, None]) & (col[None, :] < c0[:, None] + NA_WIN_W)
    pick_row = np.zeros((len(sigs), 2, key_rows, n_dr), np.float32)
    for c, (d0, d1, dsr) in enumerate(sigs):
        for qr, r0_rel in enumerate((d0, d1 + 1)):
            for kr in range(key_rows):
                krel = dsr + kr
                if r0_rel <= krel < r0_rel + kh:
                    pick_row[c, qr, kr, krel - qr + NA_WIN_H - 1] = 1.0
    row_ok = pick_row.sum(-1) > 0
    exact = lax.Precision.HIGHEST
    toe = jnp.einsum("hdj,jqk->hdqk", rpb.astype(F32), pick_col, precision=exact)
    bias = jnp.einsum("cabd,hdqk->hcaqbk", pick_row, toe, precision=exact)
    ok = row_ok[:, :, None, :, None] & col_ok[None, None, :, None, :]
    bias = jnp.where(ok[None], bias, -jnp.inf)
    return bias.reshape(rpb.shape[0], len(sigs), NA_PAIR, NA_KEYS)


def _na_kernel(start_ref, case_ref, q_ref, k_ref, v_ref, bias_ref, o_ref):
    def body(p, carry):
        qs = pl.ds(pl.multiple_of(p * NA_PAIR, NA_PAIR), NA_PAIR)
        ks = pl.ds(pl.multiple_of(start_ref[p] * GRID_W, LANES), NA_KEYS)
        q = q_ref[0, qs, :]
        sc = lax.dot_general(q, k_ref[0, ks, :], (((1,), (1,)), ((), ())), preferred_element_type=F32)
        sc = sc * (NA_DIM ** -0.5) + bias_ref[0, case_ref[p]]
        e = jnp.exp(sc - jnp.max(sc, axis=1, keepdims=True))
        pr = e / jnp.sum(e, axis=1, keepdims=True)
        o_ref[0, qs, :] = _bdot(pr.astype(BF16), v_ref[0, ks, :]).astype(BF16)
        return carry

    lax.fori_loop(0, q_ref.shape[1] // NA_PAIR, body, 0, unroll=8)


def _na_attn(ub, bias):
    b, s, _ = ub.shape
    starts, cases, sigs, _ = _na_plan(s // GRID_W)
    blk = lambda off: pl.BlockSpec((1, s, NA_DIM), lambda bi, h, st, cs: (bi, 0, off + h))
    return pl.pallas_call(
        _na_kernel,
        grid_spec=pltpu.PrefetchScalarGridSpec(
            num_scalar_prefetch=2,
            grid=(b, NA_HEADS),
            in_specs=[blk(0), blk(NA_HEADS), blk(2 * NA_HEADS),
                      pl.BlockSpec((1, len(sigs), NA_PAIR, NA_KEYS), lambda bi, h, st, cs: (h, 0, 0, 0))],
            out_specs=pl.BlockSpec((1, s, NA_DIM), lambda bi, h, st, cs: (bi, 0, h)),
        ),
        out_shape=jax.ShapeDtypeStruct((b, s, NA_HEADS * NA_DIM), BF16),
        compiler_params=_params(("arbitrary", "arbitrary")),
        name="na_attn",
    )(jnp.asarray(starts), jnp.asarray(cases), ub, ub, ub, bias)


def _oproj_kernel(oa_ref, ob_ref, oc_ref, x_ref, ga_ref, wo_ref, g1_ref, b1_ref, scf_ref, shf_ref, rw_ref,
                  x1_ref, h2_ref, lg_ref, *, alpha):
    na, nb = oa_ref.shape[2], ob_ref.shape[2]
    acc = (_bdot(oa_ref[0], wo_ref[:na, :]) + _bdot(ob_ref[0], wo_ref[na:na + nb, :])
           + _bdot(oc_ref[0], wo_ref[na + nb:, :]))
    x1 = _ln(alpha * x_ref[0] + (1.0 + ga_ref[0]) * acc) * g1_ref[...] + b1_ref[...]
    x1_ref[0] = x1
    h2 = _ln(x1) * (1.0 + scf_ref[0]) + shf_ref[0]
    h2_ref[0] = h2
    lg_ref[0] = _dot3(h2, rw_ref[...])


def _oproj(oa, ob, oc, x, ga, wo, g1, b1, scf, shf, rw, alpha):
    b, s, d = x.shape
    tm = min(ROW_TILE // 2, s)
    row = lambda w: pl.BlockSpec((1, tm, w), lambda bi, m: (bi, m, 0))
    per_b = pl.BlockSpec((1, 1, d), lambda bi, m: (bi, 0, 0))
    const = lambda shape: pl.BlockSpec(shape, lambda bi, m: (0, 0))
    return pl.pallas_call(
        functools.partial(_oproj_kernel, alpha=alpha),
        grid=(b, s // tm),
        in_specs=[row(oa.shape[2]), row(ob.shape[2]), row(oc.shape[2]), row(d), per_b,
                  const(wo.shape), const((1, d)), const((1, d)), per_b, per_b, const(rw.shape)],
        out_specs=[row(d), row(d), row(LANES)],
        out_shape=[jax.ShapeDtypeStruct((b, s, d), F32), jax.ShapeDtypeStruct((b, s, d), F32),
                   jax.ShapeDtypeStruct((b, s, LANES), F32)],
        compiler_params=_params(("arbitrary", "arbitrary")),
        name="oproj",
    )(oa, ob, oc, x, ga, wo, g1.reshape(1, d), b1.reshape(1, d), scf, shf, rw)


def _first_max(cur, idx, axis, size):
    m = jnp.max(cur, axis=axis, keepdims=True)
    first = jnp.min(jnp.where(cur == m, idx, size), axis=axis, keepdims=True)
    return idx == first, m


def _router_kernel(lg_ref, rb_ref, wt_ref, lc_ref, ki_ref):
    tn = lg_ref.shape[0]
    gsz = N_EXPERTS // N_GROUPS
    neg = -jnp.inf
    scores = _sigmoid(lg_ref[...].T[:N_EXPERTS, :])
    biased = scores + rb_ref[...]
    b3 = biased.reshape(N_GROUPS, gsz, tn)
    i3 = lax.broadcasted_iota(jnp.int32, b3.shape, 1)
    pick1, m1 = _first_max(b3, i3, 1, gsz)
    m2 = jnp.max(jnp.where(pick1, neg, b3), axis=1, keepdims=True)
    gs = (m1 + m2).reshape(N_GROUPS, tn)
    gi = lax.broadcasted_iota(jnp.int32, gs.shape, 0)
    gsel = jnp.zeros(gs.shape, F32)
    for _ in range(TOPK_GROUPS):
        pick, _m = _first_max(gs, gi, 0, N_GROUPS)
        gsel = jnp.where(pick, 1.0, gsel)
        gs = jnp.where(pick, neg, gs)
    emask = jnp.broadcast_to(gsel.reshape(N_GROUPS, 1, tn), b3.shape).reshape(N_EXPERTS, tn)
    cur = jnp.where(emask > 0.0, biased, neg)
    ei = lax.broadcasted_iota(jnp.int32, cur.shape, 0)
    self_ = jnp.zeros(cur.shape, F32)
    for _ in range(TOP_K):
        pick, _m = _first_max(cur, ei, 0, N_EXPERTS)
        self_ = jnp.where(pick, 1.0, self_)
        cur = jnp.where(pick, neg, cur)
    sel = self_ > 0.0
    w = jnp.where(sel, scores, 0.0)
    gate = w / jnp.sum(w, axis=0, keepdims=True) * ROUTED_SCALE
    selb = self_.astype(BF16)
    lower = (lax.broadcasted_iota(jnp.int32, (N_EXPERTS, N_EXPERTS), 1)
             < lax.broadcasted_iota(jnp.int32, (N_EXPERTS, N_EXPERTS), 0)).astype(BF16)
    kidx = _bdot(lower, selb)
    ki_ref[...] = jnp.where(sel, kidx + 1.0, 0.0).astype(BF16)
    ti = lax.broadcasted_iota(jnp.int32, (tn, tn), 0)
    tj = lax.broadcasted_iota(jnp.int32, (tn, tn), 1)
    same_block = (ti // TOKEN_BLOCK) == (tj // TOKEN_BLOCK)
    lc_ref[...] = _bdot(selb, jnp.logical_and(ti <= tj, same_block).astype(BF16)).astype(BF16)
    w_rows = [jnp.sum(jnp.where(sel & (kidx == float(k)), gate, 0.0), axis=0, keepdims=True)
              for k in range(TOP_K)]
    wk = jnp.concatenate(w_rows + [jnp.zeros((LANES - TOP_K, tn), F32)], axis=0)
    wt_ref[...] = wk.T


def _router(logits, router_bias):
    n = logits.shape[0]
    tn = min(ROUTER_TILE, n)
    return pl.pallas_call(
        _router_kernel,
        grid=(n // tn,),
        in_specs=[pl.BlockSpec((tn, LANES), lambda t: (t, 0)),
                  pl.BlockSpec((N_EXPERTS, 1), lambda t: (0, 0))],
        out_specs=[pl.BlockSpec((tn, LANES), lambda t: (t, 0)),
                   pl.BlockSpec((N_EXPERTS, tn), lambda t: (0, t)),
                   pl.BlockSpec((N_EXPERTS, tn), lambda t: (0, t))],
        out_shape=[jax.ShapeDtypeStruct((n, LANES), F32),
                   jax.ShapeDtypeStruct((N_EXPERTS, n), BF16),
                   jax.ShapeDtypeStruct((N_EXPERTS, n), BF16)],
        compiler_params=_params(("arbitrary",)),
        name="router",
    )(logits, router_bias.reshape(N_EXPERTS, 1))


def _slots_kernel(te_ref, r0_ref, lc_ref, ki_ref, pin_ref, pexh_ref, pexl_ref, src_ref, dst_ref, *, n):
    g = pl.program_id(0)
    tm = src_ref.shape[2]
    nb = lc_ref.shape[1]
    slot_i = lax.broadcasted_iota(jnp.int32, (tm, 1), 0).astype(F32)
    blk_i = lax.broadcasted_iota(jnp.int32, (tm, nb), 1).astype(F32)
    ones_nb = jnp.ones((nb, LANES), BF16)
    ones_tb = jnp.ones((TOKEN_BLOCK, LANES), BF16)

    def lanes(x):
        return x.T[0:1, :].astype(jnp.int32)

    for j in range(SLOT_GROUP):
        i = g * SLOT_GROUP + j
        e = te_ref[i]
        r = slot_i + r0_ref[i].astype(F32)
        blk = _bdot((pin_ref[e] <= r).astype(BF16), ones_nb)
        ohb = (blk_i == blk[:, :nb]).astype(BF16)
        table = jnp.concatenate([jnp.broadcast_to(pexh_ref[e], (nb, LANES)).astype(BF16),
                                 jnp.broadcast_to(pexl_ref[e], (nb, LANES)).astype(BF16),
                                 lc_ref[e], ki_ref[e]], axis=1)
        row = _bdot(ohb, table)
        r_loc = r - (row[:, :LANES] * PREFIX_RADIX + row[:, LANES:2 * LANES])
        lc_row = row[:, 2 * LANES:2 * LANES + TOKEN_BLOCK]
        ki_row = row[:, 2 * LANES + TOKEN_BLOCK:]
        tl = _bdot((lc_row <= r_loc).astype(BF16), ones_tb)
        kk = _bdot(jnp.where(lc_row == r_loc + 1.0, ki_row, 0.0).astype(BF16), ones_tb) - 1.0
        tok = blk * TOKEN_BLOCK + tl
        valid = blk < nb
        src_ref[j] = lanes(jnp.where(valid, tok, 0.0))
        dst_ref[j] = lanes(jnp.where(valid, kk * n + tok, 0.0))


def _moe_plan(lc, ki, n):
    tm = MOE_TILE
    n_tiles = TOP_K * n // tm + N_EXPERTS
    nb = n // TOKEN_BLOCK
    lc3 = lc.reshape(N_EXPERTS, nb, TOKEN_BLOCK)
    cb = lc3[:, :, TOKEN_BLOCK - 1].astype(F32)
    pin = jnp.cumsum(cb, axis=1)
    pex = pin - cb
    pex_hi = jnp.floor(pex / PREFIX_RADIX)
    counts = pin[:, nb - 1].astype(jnp.int32)
    ntile = (counts + tm - 1) // tm
    tile_end = jnp.cumsum(ntile)
    tile_start = tile_end - ntile
    n_used = tile_end[-1]
    ti = jnp.minimum(jnp.arange(n_tiles, dtype=jnp.int32), n_used - 1)
    te = jnp.minimum(jnp.sum(tile_end[None, :] <= ti[:, None], axis=1), N_EXPERTS - 1).astype(jnp.int32)
    onehot_te = te[:, None] == jnp.arange(N_EXPERTS, dtype=jnp.int32)[None, :]
    start_te = jnp.sum(jnp.where(onehot_te, tile_start[None, :], 0), axis=1)
    count_te = jnp.sum(jnp.where(onehot_te, counts[None, :], 0), axis=1)
    first = (jnp.arange(n_tiles, dtype=jnp.int32) == start_te).astype(jnp.int32)
    r0 = ((ti - start_te) * tm).astype(jnp.int32)
    nvalid = jnp.clip(count_te - r0, 0, tm).astype(jnp.int32)
    ar = jnp.arange(N_EXPERTS, dtype=jnp.int32)
    later_used = jnp.logical_and(ar[None, :] > ar[:, None], (counts > 0)[None, :])
    next_used = jnp.min(jnp.where(later_used, ar[None, :], N_EXPERTS), axis=1)
    nxt_e = jnp.sum(jnp.where(onehot_te, next_used[None, :], 0), axis=1).astype(jnp.int32)
    whole = lambda shape: pl.BlockSpec(shape, lambda g, *_: (0,) * len(shape))
    out_blk = pl.BlockSpec((SLOT_GROUP, 1, tm), lambda g, *_: (g, 0, 0))
    src, dst = pl.pallas_call(
        functools.partial(_slots_kernel, n=n),
        grid_spec=pltpu.PrefetchScalarGridSpec(
            num_scalar_prefetch=2,
            grid=(n_tiles // SLOT_GROUP,),
            in_specs=[whole((N_EXPERTS, nb, TOKEN_BLOCK)), whole((N_EXPERTS, nb, TOKEN_BLOCK)),
                      whole((N_EXPERTS, 1, nb)), whole((N_EXPERTS, nb, 1)), whole((N_EXPERTS, nb, 1))],
            out_specs=[out_blk, out_blk],
        ),
        out_shape=[jax.ShapeDtypeStruct((n_tiles, 1, tm), jnp.int32)] * 2,
        compiler_params=_params(("arbitrary",)),
        name="moe_slots",
    )(te, r0, lc3, ki.reshape(N_EXPERTS, nb, TOKEN_BLOCK), pin.reshape(N_EXPERTS, 1, nb),
      pex_hi.reshape(N_EXPERTS, nb, 1), (pex - pex_hi * PREFIX_RADIX).reshape(N_EXPERTS, nb, 1))
    return te, first, nvalid, nxt_e, n_used.reshape(1).astype(jnp.int32), src, dst


def _moe_kernel(te_ref, first_ref, nvalid_ref, nxte_ref, nused_ref, src_cur, src_nxt, dst_cur, dst_prv, h_hbm,
                wg_hbm, wu_hbm, wd_hbm, y_hbm, xbuf0, xbuf1, ybuf0, ybuf1, wgf, wuf, wdf, wgb, wub, wdb,
                gsem, ssem, wsem, *, layer):
    i = pl.program_id(0)
    tm = xbuf0.shape[0]
    n_used = nused_ref[0]
    nv = nvalid_ref[i]
    xbufs, ybufs = (xbuf0, xbuf1), (ybuf0, ybuf1)

    def gather(src_ref, p):
        for r in range(tm):
            pltpu.make_async_copy(h_hbm.at[pl.ds(src_ref[0, 0, r], 1)], xbufs[p].at[pl.ds(r, 1)],
                                  gsem.at[p]).start()

    def gather_wait(p):
        pltpu.make_async_copy(h_hbm.at[pl.ds(0, tm)], xbufs[p], gsem.at[p]).wait()

    def scatter_row(dst_ref, p, r):
        pltpu.make_async_copy(ybufs[p].at[pl.ds(r, 1)], y_hbm.at[pl.ds(dst_ref[0, 0, r], 1)],
                              ssem.at[p]).start()

    def scatter_wait(p, rows):
        @pl.when(rows == tm)
        def _():
            pltpu.make_async_copy(ybufs[p], y_hbm.at[pl.ds(0, tm)], ssem.at[p]).wait()

        @pl.when(rows < tm)
        def _():
            def wait_rows(count):
                def body(r, carry):
                    pltpu.make_async_copy(ybufs[p].at[pl.ds(0, count)], y_hbm.at[pl.ds(0, count)],
                                          ssem.at[p]).wait()
                    return carry
                return body

            lax.fori_loop(0, rows // ROW_GROUP, wait_rows(ROW_GROUP), 0)
            lax.fori_loop(0, rows % ROW_GROUP, wait_rows(1), 0)

    def weight_copies(e):
        return [pltpu.make_async_copy(hbm.at[layer, e], buf, wsem.at[k])
                for k, (hbm, buf) in enumerate(((wg_hbm, wgf), (wu_hbm, wuf), (wd_hbm, wdf)))]

    def step(p):
        @pl.when(i >= 2)
        def _():
            scatter_wait(p, nvalid_ref[i - 2])

        gather_wait(p)

        @pl.when(first_ref[i] == 1)
        def _():
            @pl.when(i == 0)
            def _():
                for cp in weight_copies(te_ref[0]):
                    cp.start(priority=1)

            for cp in weight_copies(te_ref[i]):
                cp.wait()
            wgb[...] = wgf[...].astype(BF16)
            wub[...] = wuf[...].astype(BF16)
            wdb[...] = wdf[...].astype(BF16)
            nxt = nxte_ref[i]

            @pl.when(nxt < N_EXPERTS)
            def _():
                for cp in weight_copies(nxt):
                    cp.start(priority=1)

        def main(scatter_prev):
            gather(src_nxt, 1 - p)
            if scatter_prev:
                for r in range(tm):
                    scatter_row(dst_prv, 1 - p, r)
            x = xbufs[p][...].astype(BF16)
            act = _silu(_bdot(x, wgb[...])) * _bdot(x, wub[...])
            ybufs[p][...] = _bdot(act.astype(BF16), wdb[...])

        prev_full = jnp.logical_and(i >= 1, nvalid_ref[jnp.maximum(i - 1, 0)] == tm)

        @pl.when(prev_full)
        def _():
            main(True)

        @pl.when(jnp.logical_not(prev_full))
        def _():
            main(False)

        last = i == n_used - 1

        @pl.when(jnp.logical_or(nv < tm, last))
        def _():
            def group_body(g, carry):
                for u in range(ROW_GROUP):
                    scatter_row(dst_cur, p, g * ROW_GROUP + u)
                return carry

            def row_body(r, carry):
                scatter_row(dst_cur, p, r)
                return carry

            groups = nv // ROW_GROUP
            lax.fori_loop(0, groups, group_body, 0)
            lax.fori_loop(groups * ROW_GROUP, nv, row_body, 0)

        @pl.when(last)
        def _():
            gather_wait(1 - p)
            scatter_wait(p, nv)

            @pl.when(i >= 1)
            def _():
                scatter_wait(1 - p, nvalid_ref[i - 1])

    @pl.when(i < n_used)
    def _():
        @pl.when(i == 0)
        def _():
            ybuf1[...] = jnp.zeros_like(ybuf1)
            tail0 = y_hbm.shape[0] - Y_TAIL_TILES * tm
            fills = [pltpu.make_async_copy(ybuf1, y_hbm.at[pl.ds(tail0 + j * tm, tm)], ssem.at[1])
                     for j in range(Y_TAIL_TILES)]
            for cp in fills:
                cp.start()
            for cp in fills:
                cp.wait()
            gather(src_cur, 0)

        for p in range(2):
            @pl.when(i % 2 == p)
            def _():
                step(p)


def _moe(h2, plan, w_gate, w_up, w_down, layer):
    n, d = h2.shape
    te, first, nvalid, nxt_e, n_used, src, dst = plan
    n_tiles, _, tm = src.shape
    f = w_gate.shape[3]
    idx_blk = lambda fn: pl.BlockSpec((1, 1, tm), fn, memory_space=pltpu.SMEM)
    hbm = pl.BlockSpec(memory_space=pl.ANY)
    return pl.pallas_call(
        functools.partial(_moe_kernel, layer=layer),
        grid_spec=pltpu.PrefetchScalarGridSpec(
            num_scalar_prefetch=5,
            grid=(n_tiles,),
            in_specs=[idx_blk(lambda i, *_: (i, 0, 0)),
                      idx_blk(lambda i, *_: (jnp.minimum(i + 1, n_tiles - 1), 0, 0)),
                      idx_blk(lambda i, *_: (i, 0, 0)),
                      idx_blk(lambda i, *_: (jnp.maximum(i - 1, 0), 0, 0)),
                      hbm, hbm, hbm, hbm],
            out_specs=hbm,
            scratch_shapes=[pltpu.VMEM((tm, d), F32)] * 4
                           + [pltpu.VMEM((d, f), F32), pltpu.VMEM((d, f), F32), pltpu.VMEM((f, d), F32),
                              pltpu.VMEM((d, f), BF16), pltpu.VMEM((d, f), BF16), pltpu.VMEM((f, d), BF16),
                              pltpu.SemaphoreType.DMA((2,)), pltpu.SemaphoreType.DMA((2,)),
                              pltpu.SemaphoreType.DMA((3,))],
        ),
        out_shape=jax.ShapeDtypeStruct((TOP_K * n + Y_TAIL_TILES * tm, d), F32),
        compiler_params=_params(("arbitrary",)),
        name="moe_experts",
    )(te, first, nvalid, nxt_e, n_used, src, src, dst, dst, h2, w_gate, w_up, w_down)


def _combine_kernel(*refs, alpha):
    y_refs = refs[:TOP_K]
    wt_ref, h_ref, x_ref, gf_ref, sg_ref, su_ref, sd_ref, g2_ref, b2_ref, o_ref = refs[TOP_K:]
    wt = wt_ref[...]
    routed = wt[:, 0:1] * y_refs[0][...]
    for k in range(1, TOP_K):
        routed = routed + wt[:, k:k + 1] * y_refs[k][...]
    hb = h_ref[...].astype(BF16)
    act = _silu(_bdot(hb, sg_ref[...])) * _bdot(hb, su_ref[...])
    y = routed + _bdot(act.astype(BF16), sd_ref[...])
    o_ref[...] = _ln(alpha * x_ref[...] + (1.0 + gf_ref[0]) * y) * g2_ref[...] + b2_ref[...]


def _combine(y_tok, wt, h2, x1, gf, sg, su, sd, g2, b2, alpha, seq):
    n, d = h2.shape
    tm = min(COMBINE_TILE, seq)
    nblk = n // tm
    row = pl.BlockSpec((tm, d), lambda i: (i, 0))
    const = lambda shape: pl.BlockSpec(shape, lambda i: (0, 0))
    y_specs = [pl.BlockSpec((tm, d), (lambda i, k=k: (k * nblk + i, 0))) for k in range(TOP_K)]
    return pl.pallas_call(
        functools.partial(_combine_kernel, alpha=alpha),
        grid=(nblk,),
        in_specs=y_specs + [pl.BlockSpec((tm, LANES), lambda i: (i, 0)), row, row,
                            pl.BlockSpec((1, 1, d), lambda i: (i // (seq // tm), 0, 0)),
                            const(sg.shape), const(su.shape), const(sd.shape), const((1, d)), const((1, d))],
        out_specs=row,
        out_shape=jax.ShapeDtypeStruct((n, d), F32),
        compiler_params=_params(("arbitrary",)),
        name="combine",
    )(*([y_tok] * TOP_K), wt, h2, x1, gf, sg, su, sd, g2.reshape(1, d), b2.reshape(1, d))


def _rope_tables(s, half, width, offset, period=None):
    inv = ROPE_THETA ** (-jnp.arange(half, dtype=F32) / half)
    ang = jnp.arange(s, dtype=jnp.int32).astype(F32)[:, None] * inv[None, :]
    cos, sin = jnp.cos(ang), jnp.sin(ang)
    period = period or width
    tail = period - offset - 2 * half
    one = lambda n: jnp.ones((s, n), F32)
    zero = lambda n: jnp.zeros((s, n), F32)
    c = jnp.concatenate([one(offset), cos, cos, one(tail)], axis=1)
    s1 = jnp.concatenate([zero(offset), -sin, zero(half + tail)], axis=1)
    s2 = jnp.concatenate([zero(offset + half), sin, zero(tail)], axis=1)
    reps = width // period
    return tuple(jnp.tile(t, (1, reps)) for t in (c, s1, s2))


def kernel(x, c, w_ada, b_ada, w_in, mla_q_norm, mla_w_uq, mla_kv_norm, mla_w_ukv, na_rpb, diff_lq1, diff_lk1,
           diff_lq2, diff_lk2, diff_subln, w_o, ln1_g, ln1_b, router_w, router_bias, exp_w_gate, exp_w_up,
           exp_w_down, sh_w_gate, sh_w_up, sh_w_down, ln2_g, ln2_b):
    b, s, d = x.shape
    depth = w_ada.shape[0]
    n = b * s
    alpha = (2 * depth) ** 0.25
    ada = _ada(c, w_ada, b_ada)
    tab_diff = _rope_tables(s, DIFF_ROT // 2, LANES, 0, period=DIFF_QK)
    tab_q = _rope_tables(s, MLA_ROPE // 2, MLA_HEAD_PAD, MLA_NOPE)
    tab_k = _rope_tables(s, MLA_ROPE // 2, LANES, 0)
    a_pad = LANES - MLA_ROPE

    for i in range(depth):
        lam_init = 0.8 - 0.6 * math.exp(-0.3 * i)
        sh_a, sc_a, g_a, sh_f, sc_f, g_f = [a[:, None, :] for a in jnp.split(ada[i], 6, -1)]
        wi = w_in[i]
        w_a = jnp.pad(wi[:, :IN_A], ((0, 0), (0, a_pad))).astype(BF16)
        w_b = wi[:, IN_A:IN_A + IN_B].astype(BF16)
        w_c = wi[:, IN_A + IN_B:].astype(BF16)
        wq = jnp.pad(mla_w_uq[i].reshape(MLA_Q_RANK, MLA_HEADS, MLA_NOPE + MLA_ROPE),
                     ((0, 0), (0, 0), (0, MLA_HEAD_PAD - MLA_NOPE - MLA_ROPE)))
        wq = wq.reshape(MLA_Q_RANK, MLA_HEADS * MLA_HEAD_PAD).astype(BF16)
        wkv3 = mla_w_ukv[i].reshape(MLA_KV_RANK, MLA_HEADS, MLA_NOPE + MLA_V)
        wkv = jnp.concatenate([wkv3[:, :, :MLA_NOPE].reshape(MLA_KV_RANK, -1),
                               wkv3[:, :, MLA_NOPE:].reshape(MLA_KV_RANK, -1)], axis=1).astype(BF16)
        rw = jnp.pad(router_w[i], ((0, 0), (0, LANES - N_EXPERTS)))

        ua, ub, uc, kt_c = _inproj(x, sc_a, sh_a, w_a, w_b, w_c, tab_diff)
        q_a, kt_a, v_a = _mla_proj(ua, mla_q_norm[i], mla_kv_norm[i], wq, wkv, tab_q, tab_k)
        o_a = _mla_attn(q_a, kt_a, v_a)
        o_b = _na_attn(ub, _na_bias(na_rpb[i], s // GRID_W))
        o_c = _diff_attn(uc, kt_c, diff_lq1[i], diff_lk1[i], diff_lq2[i], diff_lk2[i], diff_subln[i], lam_init)
        x1, h2, logits = _oproj(o_a, o_b, o_c, x, g_a, w_o[i].astype(BF16), ln1_g[i], ln1_b[i], sc_f, sh_f,
                                rw, alpha)

        h2 = h2.reshape(n, d)
        wt, lc, ki = _router(logits.reshape(n, LANES), router_bias[i])
        plan = _moe_plan(lc, ki, n)
        y_tok = _moe(h2, plan, exp_w_gate, exp_w_up, exp_w_down, i)
        x = _combine(y_tok, wt, h2, x1.reshape(n, d), g_f, sh_w_gate[i].astype(BF16), sh_w_up[i].astype(BF16),
                     sh_w_down[i].astype(BF16), ln2_g[i], ln2_b[i], alpha, s).reshape(b, s, d)
    return x
```

```python
import functools
import math

import numpy as np
import jax
import jax.numpy as jnp
from jax import lax
from jax.experimental import pallas as pl
from jax.experimental.pallas import tpu as pltpu

F32 = jnp.float32
BF16 = jnp.bfloat16

GRID_W = 64
ROPE_THETA = 500000.0
MLA_HEADS, MLA_Q_RANK, MLA_KV_RANK, MLA_NOPE, MLA_ROPE, MLA_V = 8, 512, 256, 128, 64, 128
NA_HEADS, NA_DIM, NA_WIN_H, NA_WIN_W = 4, 128, 8, 16
DIFF_HEADS, DIFF_QK = 4, 64
DIFF_V = 2 * DIFF_QK
DIFF_ROT = DIFF_QK // 4
IN_A = MLA_Q_RANK + MLA_KV_RANK + MLA_ROPE
IN_B = 3 * NA_HEADS * NA_DIM
N_EXPERTS, TOP_K, N_GROUPS, TOPK_GROUPS = 64, 8, 8, 4
ROUTED_SCALE = 2.5
LN_EPS = 1e-5
RMS_EPS = 1e-6

LANES = 128
MLA_HEAD_PAD = 256
VMEM_LIMIT = 56 * 1024 * 1024
ROW_TILE = 512
MLA_Q_TILE = 1024
DIFF_Q_TILE = 512
ATTN_KV_CHUNK = 256
NA_PAIR = 2 * GRID_W
NA_KEYS = 10 * GRID_W
MOE_TILE = 256
ROW_GROUP = 8
TOKEN_BLOCK = LANES
SLOT_GROUP = 8
PREFIX_RADIX = 256.0
COMBINE_TILE = 128
Y_TAIL_TILES = 8
ROUTER_TILE = 512


def _params(sem, vmem=VMEM_LIMIT):
    return pltpu.CompilerParams(dimension_semantics=sem, vmem_limit_bytes=vmem)


def _sigmoid(x):
    return 1.0 / (1.0 + jnp.exp(-x))


def _silu(x):
    return x * _sigmoid(x)


def _ln(x):
    mu = jnp.mean(x, axis=-1, keepdims=True)
    xc = x - mu
    var = jnp.mean(xc * xc, axis=-1, keepdims=True)
    return xc * lax.rsqrt(var + LN_EPS)


def _rms(x, g):
    return x * lax.rsqrt(jnp.mean(x * x, axis=-1, keepdims=True) + RMS_EPS) * g


def _bdot(a, b):
    return jnp.dot(a, b, preferred_element_type=F32)


def _split(a):
    hi = a.astype(BF16)
    lo = (a - hi.astype(F32)).astype(BF16)
    return hi, lo


def _dot3(a, b):
    ah, al = _split(a)
    bh, bl = _split(b)
    return _bdot(ah, bh) + (_bdot(al, bh) + _bdot(ah, bl))


def _rope(x, c, s1, s2, half):
    width = x.shape[1]
    reps = width // c.shape[1]
    if reps > 1:
        c, s1, s2 = (jnp.concatenate([t] * reps, axis=1) for t in (c, s1, s2))
    return x * c + pltpu.roll(x, width - half, 1) * s1 + pltpu.roll(x, half, 1) * s2


def _ada_kernel(c_ref, w_ref, b_ref, o_ref):
    o_ref[0] = _dot3(_silu(c_ref[...]), w_ref[0]) + b_ref[0]


def _ada(c, w_ada, b_ada):
    depth, d, d6 = w_ada.shape
    batch = c.shape[0]
    rows = 8
    cp = jnp.pad(c, ((0, rows - batch), (0, 0)))
    tn = 512
    out = pl.pallas_call(
        _ada_kernel,
        grid=(depth, d6 // tn),
        in_specs=[pl.BlockSpec((rows, d), lambda l, n: (0, 0)),
                  pl.BlockSpec((1, d, tn), lambda l, n: (l, 0, n)),
                  pl.BlockSpec((1, 1, tn), lambda l, n: (l, 0, n))],
        out_specs=pl.BlockSpec((1, rows, tn), lambda l, n: (l, 0, n)),
        out_shape=jax.ShapeDtypeStruct((depth, rows, d6), F32),
        compiler_params=_params(("arbitrary", "arbitrary")),
        name="ada",
    )(cp, w_ada, b_ada.reshape(depth, 1, d6))
    return out[:, :batch]


def _inproj_kernel(x_ref, sc_ref, sh_ref, wa_ref, wb_ref, wc_ref, c_ref, s1_ref, s2_ref,
                   ua_ref, ub_ref, qv_ref, kt_ref):
    nq = DIFF_HEADS * 2 * DIFF_QK
    h = (_ln(x_ref[0]) * (1.0 + sc_ref[0]) + sh_ref[0]).astype(BF16)
    ua_ref[0] = _bdot(h, wa_ref[...])
    ub_ref[0] = _bdot(h, wb_ref[...]).astype(BF16)
    acc = _bdot(h, wc_ref[...])
    r = _rope(acc[:, :2 * nq], c_ref[...], s1_ref[...], s2_ref[...], DIFF_ROT // 2)
    qv_ref[0, :, :nq] = (r[:, :nq] * (DIFF_QK ** -0.5)).astype(BF16)
    qv_ref[0, :, nq:] = acc[:, 2 * nq:].astype(BF16)
    for hd in range(DIFF_HEADS):
        kt_ref[0, hd] = r[:, nq + hd * DIFF_V:nq + (hd + 1) * DIFF_V].T.astype(BF16)


def _inproj(x, sc, sh, wa, wb, wc, tables):
    b, s, d = x.shape
    tm = min(ROW_TILE, s)
    n_qv = 2 * DIFF_HEADS * DIFF_V
    row = lambda w: pl.BlockSpec((1, tm, w), lambda bi, m: (bi, m, 0))
    per_b = pl.BlockSpec((1, 1, d), lambda bi, m: (bi, 0, 0))
    weight = lambda w: pl.BlockSpec(w.shape, lambda bi, m: (0, 0), pipeline_mode=pl.Buffered(1))
    return pl.pallas_call(
        _inproj_kernel,
        grid=(b, s // tm),
        in_specs=[row(d), per_b, per_b, weight(wa), weight(wb), weight(wc)]
                 + [pl.BlockSpec((tm, LANES), lambda bi, m: (m, 0))] * 3,
        out_specs=[row(wa.shape[1]), row(wb.shape[1]), row(n_qv),
                   pl.BlockSpec((1, DIFF_HEADS, DIFF_V, tm), lambda bi, m: (bi, 0, 0, m))],
        out_shape=[jax.ShapeDtypeStruct((b, s, wa.shape[1]), F32),
                   jax.ShapeDtypeStruct((b, s, wb.shape[1]), BF16),
                   jax.ShapeDtypeStruct((b, s, n_qv), BF16),
                   jax.ShapeDtypeStruct((b, DIFF_HEADS, DIFF_V, s), BF16)],
        compiler_params=_params(("arbitrary", "arbitrary")),
        name="inproj",
    )(x, sc, sh, wa, wb, wc, *tables)


def _mla_proj_kernel(ua_ref, qn_ref, kvn_ref, wq_ref, wkv_ref, cq_ref, sq1_ref, sq2_ref,
                     ck_ref, sk1_ref, sk2_ref, q_ref, kt_ref, v_ref):
    ua = ua_ref[0]
    kv_lo = MLA_Q_RANK + MLA_KV_RANK
    cq = _rms(ua[:, :MLA_Q_RANK], qn_ref[...])
    q = _bdot(cq.astype(BF16), wq_ref[...])
    q = _rope(q, cq_ref[...], sq1_ref[...], sq2_ref[...], MLA_ROPE // 2)
    q_ref[0] = (q * ((MLA_NOPE + MLA_ROPE) ** -0.5)).astype(BF16)
    ckv = _rms(ua[:, MLA_Q_RANK:kv_lo], kvn_ref[...])
    kv = _bdot(ckv.astype(BF16), wkv_ref[...])
    nk = MLA_HEADS * MLA_NOPE
    v_ref[0] = kv[:, nk:].astype(BF16)
    kr = _rope(ua[:, kv_lo:], ck_ref[...], sk1_ref[...], sk2_ref[...], MLA_ROPE // 2)
    krt = kr.T.astype(BF16)
    for h in range(MLA_HEADS):
        kt_ref[0, h, :MLA_NOPE, :] = kv[:, h * MLA_NOPE:(h + 1) * MLA_NOPE].T.astype(BF16)
        kt_ref[0, h, MLA_NOPE:, :] = krt


def _mla_proj(ua, q_norm, kv_norm, wq, wkv, tq, tk):
    b, s, wa = ua.shape
    tm = min(ROW_TILE, s)
    hq = MLA_HEADS * MLA_HEAD_PAD
    hv = MLA_HEADS * MLA_V
    const = lambda bi, m: (0, 0)
    tab = lambda w: pl.BlockSpec((tm, w), lambda bi, m: (m, 0))
    return pl.pallas_call(
        _mla_proj_kernel,
        grid=(b, s // tm),
        in_specs=[pl.BlockSpec((1, tm, wa), lambda bi, m: (bi, m, 0)),
                  pl.BlockSpec((1, MLA_Q_RANK), const),
                  pl.BlockSpec((1, MLA_KV_RANK), const),
                  pl.BlockSpec(wq.shape, const),
                  pl.BlockSpec(wkv.shape, const),
                  tab(MLA_HEAD_PAD), tab(MLA_HEAD_PAD), tab(MLA_HEAD_PAD),
                  tab(LANES), tab(LANES), tab(LANES)],
        out_specs=[pl.BlockSpec((1, tm, hq), lambda bi, m: (bi, m, 0)),
                   pl.BlockSpec((1, MLA_HEADS, MLA_HEAD_PAD, tm), lambda bi, m: (bi, 0, 0, m)),
                   pl.BlockSpec((1, tm, hv), lambda bi, m: (bi, m, 0))],
        out_shape=[jax.ShapeDtypeStruct((b, s, hq), BF16),
                   jax.ShapeDtypeStruct((b, MLA_HEADS, MLA_HEAD_PAD, s), BF16),
                   jax.ShapeDtypeStruct((b, s, hv), BF16)],
        compiler_params=_params(("arbitrary", "arbitrary")),
        name="mla_proj",
    )(ua, q_norm.reshape(1, -1), kv_norm.reshape(1, -1), wq, wkv, *tq, *tk)


class _Softmax:
    def __init__(self):
        self.m = self.acc = None

    def update(self, sc, v_ones):
        mc = jnp.max(sc, axis=1, keepdims=True)
        m_new = mc if self.m is None else jnp.maximum(self.m, mc)
        pv = _bdot(jnp.exp((sc - m_new).astype(BF16)), v_ones)
        self.acc = pv if self.m is None else jnp.exp(self.m - m_new) * self.acc + pv
        self.m = m_new

    def result(self):
        dv = self.acc.shape[1] - LANES
        return self.acc[:, :dv] * (1.0 / self.acc[:, dv:dv + 1])


def _kv_chunks(s):
    kc = min(ATTN_KV_CHUNK, s)
    return [(c * kc, (c + 1) * kc) for c in range(s // kc)]


def _with_ones(v):
    return jnp.concatenate([v, jnp.ones((v.shape[0], LANES), v.dtype)], axis=1)


def _mla_attn_kernel(q_ref, kt_ref, v_ref, o_ref):
    q = q_ref[0]
    sm = _Softmax()
    for lo, hi in _kv_chunks(kt_ref.shape[3]):
        sm.update(_bdot(q, kt_ref[0, 0, :, lo:hi]), _with_ones(v_ref[0, lo:hi, :]))
    o_ref[0] = sm.result().astype(BF16)


def _mla_attn(q, kt, v):
    b, s, _ = q.shape
    tq = min(MLA_Q_TILE, s)
    return pl.pallas_call(
        _mla_attn_kernel,
        grid=(b, MLA_HEADS, s // tq),
        in_specs=[pl.BlockSpec((1, tq, MLA_HEAD_PAD), lambda bi, h, i: (bi, i, h)),
                  pl.BlockSpec((1, 1, MLA_HEAD_PAD, s), lambda bi, h, i: (bi, h, 0, 0)),
                  pl.BlockSpec((1, s, MLA_V), lambda bi, h, i: (bi, 0, h))],
        out_specs=pl.BlockSpec((1, tq, MLA_V), lambda bi, h, i: (bi, i, h)),
        out_shape=jax.ShapeDtypeStruct((b, s, MLA_HEADS * MLA_V), BF16),
        compiler_params=_params(("arbitrary", "arbitrary", "arbitrary")),
        name="mla_attn",
    )(q, kt, v)


def _diff_attn_kernel(q_ref, kt_ref, v_ref, lq1_ref, lk1_ref, lq2_ref, lk2_ref, sub_ref, o_ref, *, lam_init):
    q = q_ref[0]
    lane = lax.broadcasted_iota(jnp.int32, q.shape, 1)
    zero = jnp.zeros_like(q)
    q1 = jnp.where(lane < DIFF_QK, q, zero)
    q2 = jnp.where(lane >= DIFF_QK, q, zero)
    lam = (jnp.exp(jnp.sum(lq1_ref[...] * lk1_ref[...], axis=1, keepdims=True))
           - jnp.exp(jnp.sum(lq2_ref[...] * lk2_ref[...], axis=1, keepdims=True)) + lam_init)
    sm1, sm2 = _Softmax(), _Softmax()
    for lo, hi in _kv_chunks(kt_ref.shape[3]):
        kt = kt_ref[0, 0, :, lo:hi]
        v = _with_ones(v_ref[0, lo:hi, :])
        sm1.update(_bdot(q1, kt), v)
        sm2.update(_bdot(q2, kt), v)
    o = sm1.result() - lam * sm2.result()
    o_ref[0] = (_rms(o, sub_ref[...]) * (1.0 - lam_init)).astype(BF16)


def _diff_attn(uc, kt, lq1, lk1, lq2, lk2, subln, lam_init):
    b, s, _ = uc.shape
    tq = min(DIFF_Q_TILE, s)
    vec = lambda n: pl.BlockSpec((1, n), lambda bi, h, i: (0, 0))
    v_block0 = DIFF_HEADS
    return pl.pallas_call(
        functools.partial(_diff_attn_kernel, lam_init=lam_init),
        grid=(b, DIFF_HEADS, s // tq),
        in_specs=[pl.BlockSpec((1, tq, DIFF_V), lambda bi, h, i: (bi, i, h)),
                  pl.BlockSpec((1, 1, 2 * DIFF_QK, s), lambda bi, h, i: (bi, h, 0, 0)),
                  pl.BlockSpec((1, s, DIFF_V), lambda bi, h, i: (bi, 0, v_block0 + h)),
                  vec(DIFF_QK), vec(DIFF_QK), vec(DIFF_QK), vec(DIFF_QK), vec(DIFF_V)],
        out_specs=pl.BlockSpec((1, tq, DIFF_V), lambda bi, h, i: (bi, i, h)),
        out_shape=jax.ShapeDtypeStruct((b, s, DIFF_HEADS * DIFF_V), BF16),
        compiler_params=_params(("arbitrary", "arbitrary", "arbitrary")),
        name="diff_attn",
    )(uc, kt, uc, lq1.reshape(1, -1), lk1.reshape(1, -1), lq2.reshape(1, -1), lk2.reshape(1, -1),
      subln.reshape(1, -1))


def _na_plan(rows):
    kh = min(NA_WIN_H, rows)
    key_rows = NA_KEYS // GRID_W
    r0 = lambda r: min(max(r - kh // 2, 0), rows - kh)
    starts, cases, sigs = [], [], []
    for p in range(rows // 2):
        sr = min(r0(2 * p), rows - key_rows)
        sig = (r0(2 * p) - 2 * p, r0(2 * p + 1) - 2 * p - 1, sr - 2 * p)
        if sig not in sigs:
            sigs.append(sig)
        starts.append(sr)
        cases.append(sigs.index(sig))
    return np.array(starts, np.int32), np.array(cases, np.int32), sigs, kh


def _na_bias(rpb, rows):
    _, _, sigs, kh = _na_plan(rows)
    n_dr, n_dc = 2 * NA_WIN_H - 1, 2 * NA_WIN_W - 1
    key_rows = NA_KEYS // GRID_W
    col = np.arange(GRID_W)
    dc = np.clip(col[None, :] - col[:, None], -(NA_WIN_W - 1), NA_WIN_W - 1) + (NA_WIN_W - 1)
    pick_col = (dc[None] == np.arange(n_dc)[:, None, None]).astype(np.float32)
    c0 = np.clip(col - NA_WIN_W // 2, 0, GRID_W - NA_WIN_W)
    col_ok = (col[None, :] >= c0[:, None]) & (col[None, :] < c0[:, None] + NA_WIN_W)
    pick_row = np.zeros((len(sigs), 2, key_rows, n_dr), np.float32)
    for c, (d0, d1, dsr) in enumerate(sigs):
        for qr, r0_rel in enumerate((d0, d1 + 1)):
            for kr in range(key_rows):
                krel = dsr + kr
                if r0_rel <= krel < r0_rel + kh:
                    pick_row[c, qr, kr, krel - qr + NA_WIN_H - 1] = 1.0
    row_ok = pick_row.sum(-1) > 0
    exact = lax.Precision.HIGHEST
    toe = jnp.einsum("hdj,jqk->hdqk", rpb.astype(F32), pick_col, precision=exact)
    bias = jnp.einsum("cabd,hdqk->hcaqbk", pick_row, toe, precision=exact)
    ok = row_ok[:, :, None, :, None] & col_ok[None, None, :, None, :]
    bias = jnp.where(ok[None], bias, -jnp.inf)
    return bias.reshape(rpb.shape[0], len(sigs), NA_PAIR, NA_KEYS)


def _na_kernel(start_ref, case_ref, q_ref, k_ref, v_ref, bias_ref, o_ref):
    def body(p, carry):
        qs = pl.ds(pl.multiple_of(p * NA_PAIR, NA_PAIR), NA_PAIR)
        ks = pl.ds(pl.multiple_of(start_ref[p] * GRID_W, LANES), NA_KEYS)
        q = q_ref[0, qs, :]
        sc = lax.dot_general(q, k_ref[0, ks, :], (((1,), (1,)), ((), ())), preferred_element_type=F32)
        sc = sc * (NA_DIM ** -0.5) + bias_ref[0, case_ref[p]]
        e = jnp.exp(sc - jnp.max(sc, axis=1, keepdims=True))
        pr = e / jnp.sum(e, axis=1, keepdims=True)
        o_ref[0, qs, :] = _bdot(pr.astype(BF16), v_ref[0, ks, :]).astype(BF16)
        return carry

    lax.fori_loop(0, q_ref.shape[1] // NA_PAIR, body, 0, unroll=8)


def _na_attn(ub, bias):
    b, s, _ = ub.shape
    starts, cases, sigs, _ = _na_plan(s // GRID_W)
    blk = lambda off: pl.BlockSpec((1, s, NA_DIM), lambda bi, h, st, cs: (bi, 0, off + h))
    return pl.pallas_call(
        _na_kernel,
        grid_spec=pltpu.PrefetchScalarGridSpec(
            num_scalar_prefetch=2,
            grid=(b, NA_HEADS),
            in_specs=[blk(0), blk(NA_HEADS), blk(2 * NA_HEADS),
                      pl.BlockSpec((1, len(sigs), NA_PAIR, NA_KEYS), lambda bi, h, st, cs: (h, 0, 0, 0))],
            out_specs=pl.BlockSpec((1, s, NA_DIM), lambda bi, h, st, cs: (bi, 0, h)),
        ),
        out_shape=jax.ShapeDtypeStruct((b, s, NA_HEADS * NA_DIM), BF16),
        compiler_params=_params(("arbitrary", "arbitrary")),
        name="na_attn",
    )(jnp.asarray(starts), jnp.asarray(cases), ub, ub, ub, bias)


def _oproj_kernel(oa_ref, ob_ref, oc_ref, x_ref, ga_ref, wo_ref, g1_ref, b1_ref, scf_ref, shf_ref, rw_ref,
                  x1_ref, h2_ref, lg_ref, *, alpha):
    na, nb = oa_ref.shape[2], ob_ref.shape[2]
    acc = (_bdot(oa_ref[0], wo_ref[:na, :]) + _bdot(ob_ref[0], wo_ref[na:na + nb, :])
           + _bdot(oc_ref[0], wo_ref[na + nb:, :]))
    x1 = _ln(alpha * x_ref[0] + (1.0 + ga_ref[0]) * acc) * g1_ref[...] + b1_ref[...]
    x1_ref[0] = x1
    h2 = _ln(x1) * (1.0 + scf_ref[0]) + shf_ref[0]
    h2_ref[0] = h2
    lg_ref[0] = _dot3(h2, rw_ref[...])


def _oproj(oa, ob, oc, x, ga, wo, g1, b1, scf, shf, rw, alpha):
    b, s, d = x.shape
    tm = min(ROW_TILE // 2, s)
    row = lambda w: pl.BlockSpec((1, tm, w), lambda bi, m: (bi, m, 0))
    per_b = pl.BlockSpec((1, 1, d), lambda bi, m: (bi, 0, 0))
    const = lambda shape: pl.BlockSpec(shape, lambda bi, m: (0, 0))
    return pl.pallas_call(
        functools.partial(_oproj_kernel, alpha=alpha),
        grid=(b, s // tm),
        in_specs=[row(oa.shape[2]), row(ob.shape[2]), row(oc.shape[2]), row(d), per_b,
                  const(wo.shape), const((1, d)), const((1, d)), per_b, per_b, const(rw.shape)],
        out_specs=[row(d), row(d), row(LANES)],
        out_shape=[jax.ShapeDtypeStruct((b, s, d), F32), jax.ShapeDtypeStruct((b, s, d), F32),
                   jax.ShapeDtypeStruct((b, s, LANES), F32)],
        compiler_params=_params(("arbitrary", "arbitrary")),
        name="oproj",
    )(oa, ob, oc, x, ga, wo, g1.reshape(1, d), b1.reshape(1, d), scf, shf, rw)


def _first_max(cur, idx, axis, size):
    m = jnp.max(cur, axis=axis, keepdims=True)
    first = jnp.min(jnp.where(cur == m, idx, size), axis=axis, keepdims=True)
    return idx == first, m


def _router_kernel(lg_ref, rb_ref, wt_ref, lc_ref, ki_ref):
    tn = lg_ref.shape[0]
    gsz = N_EXPERTS // N_GROUPS
    neg = -jnp.inf
    scores = _sigmoid(lg_ref[...].T[:N_EXPERTS, :])
    biased = scores + rb_ref[...]
    b3 = biased.reshape(N_GROUPS, gsz, tn)
    i3 = lax.broadcasted_iota(jnp.int32, b3.shape, 1)
    pick1, m1 = _first_max(b3, i3, 1, gsz)
    m2 = jnp.max(jnp.where(pick1, neg, b3), axis=1, keepdims=True)
    gs = (m1 + m2).reshape(N_GROUPS, tn)
    gi = lax.broadcasted_iota(jnp.int32, gs.shape, 0)
    gsel = jnp.zeros(gs.shape, F32)
    for _ in range(TOPK_GROUPS):
        pick, _m = _first_max(gs, gi, 0, N_GROUPS)
        gsel = jnp.where(pick, 1.0, gsel)
        gs = jnp.where(pick, neg, gs)
    emask = jnp.broadcast_to(gsel.reshape(N_GROUPS, 1, tn), b3.shape).reshape(N_EXPERTS, tn)
    cur = jnp.where(emask > 0.0, biased, neg)
    ei = lax.broadcasted_iota(jnp.int32, cur.shape, 0)
    self_ = jnp.zeros(cur.shape, F32)
    for _ in range(TOP_K):
        pick, _m = _first_max(cur, ei, 0, N_EXPERTS)
        self_ = jnp.where(pick, 1.0, self_)
        cur = jnp.where(pick, neg, cur)
    sel = self_ > 0.0
    w = jnp.where(sel, scores, 0.0)
    gate = w / jnp.sum(w, axis=0, keepdims=True) * ROUTED_SCALE
    selb = self_.astype(BF16)
    lower = (lax.broadcasted_iota(jnp.int32, (N_EXPERTS, N_EXPERTS), 1)
             < lax.broadcasted_iota(jnp.int32, (N_EXPERTS, N_EXPERTS), 0)).astype(BF16)
    kidx = _bdot(lower, selb)
    ki_ref[...] = jnp.where(sel, kidx + 1.0, 0.0).astype(BF16)
    ti = lax.broadcasted_iota(jnp.int32, (tn, tn), 0)
    tj = lax.broadcasted_iota(jnp.int32, (tn, tn), 1)
    same_block = (ti // TOKEN_BLOCK) == (tj // TOKEN_BLOCK)
    lc_ref[...] = _bdot(selb, jnp.logical_and(ti <= tj, same_block).astype(BF16)).astype(BF16)
    w_rows = [jnp.sum(jnp.where(sel & (kidx == float(k)), gate, 0.0), axis=0, keepdims=True)
              for k in range(TOP_K)]
    wk = jnp.concatenate(w_rows + [jnp.zeros((LANES - TOP_K, tn), F32)], axis=0)
    wt_ref[...] = wk.T


def _router(logits, router_bias):
    n = logits.shape[0]
    tn = min(ROUTER_TILE, n)
    return pl.pallas_call(
        _router_kernel,
        grid=(n // tn,),
        in_specs=[pl.BlockSpec((tn, LANES), lambda t: (t, 0)),
                  pl.BlockSpec((N_EXPERTS, 1), lambda t: (0, 0))],
        out_specs=[pl.BlockSpec((tn, LANES), lambda t: (t, 0)),
                   pl.BlockSpec((N_EXPERTS, tn), lambda t: (0, t)),
                   pl.BlockSpec((N_EXPERTS, tn), lambda t: (0, t))],
        out_shape=[jax.ShapeDtypeStruct((n, LANES), F32),
                   jax.ShapeDtypeStruct((N_EXPERTS, n), BF16),
                   jax.ShapeDtypeStruct((N_EXPERTS, n), BF16)],
        compiler_params=_params(("arbitrary",)),
        name="router",
    )(logits, router_bias.reshape(N_EXPERTS, 1))


def _slots_kernel(te_ref, r0_ref, lc_ref, ki_ref, pin_ref, pexh_ref, pexl_ref, src_ref, dst_ref, *, n):
    g = pl.program_id(0)
    tm = src_ref.shape[2]
    nb = lc_ref.shape[1]
    slot_i = lax.broadcasted_iota(jnp.int32, (tm, 1), 0).astype(F32)
    blk_i = lax.broadcasted_iota(jnp.int32, (tm, nb), 1).astype(F32)
    ones_nb = jnp.ones((nb, LANES), BF16)
    ones_tb = jnp.ones((TOKEN_BLOCK, LANES), BF16)

    def lanes(x):
        return x.T[0:1, :].astype(jnp.int32)

    for j in range(SLOT_GROUP):
        i = g * SLOT_GROUP + j
        e = te_ref[i]
        r = slot_i + r0_ref[i].astype(F32)
        blk = _bdot((pin_ref[e] <= r).astype(BF16), ones_nb)
        ohb = (blk_i == blk[:, :nb]).astype(BF16)
        table = jnp.concatenate([jnp.broadcast_to(pexh_ref[e], (nb, LANES)).astype(BF16),
                                 jnp.broadcast_to(pexl_ref[e], (nb, LANES)).astype(BF16),
                                 lc_ref[e], ki_ref[e]], axis=1)
        row = _bdot(ohb, table)
        r_loc = r - (row[:, :LANES] * PREFIX_RADIX + row[:, LANES:2 * LANES])
        lc_row = row[:, 2 * LANES:2 * LANES + TOKEN_BLOCK]
        ki_row = row[:, 2 * LANES + TOKEN_BLOCK:]
        tl = _bdot((lc_row <= r_loc).astype(BF16), ones_tb)
        kk = _bdot(jnp.where(lc_row == r_loc + 1.0, ki_row, 0.0).astype(BF16), ones_tb) - 1.0
        tok = blk * TOKEN_BLOCK + tl
        valid = blk < nb
        src_ref[j] = lanes(jnp.where(valid, tok, 0.0))
        dst_ref[j] = lanes(jnp.where(valid, kk * n + tok, 0.0))


def _moe_plan(lc, ki, n):
    tm = MOE_TILE
    n_tiles = TOP_K * n // tm + N_EXPERTS
    nb = n // TOKEN_BLOCK
    lc3 = lc.reshape(N_EXPERTS, nb, TOKEN_BLOCK)
    cb = lc3[:, :, TOKEN_BLOCK - 1].astype(F32)
    pin = jnp.cumsum(cb, axis=1)
    pex = pin - cb
    pex_hi = jnp.floor(pex / PREFIX_RADIX)
    counts = pin[:, nb - 1].astype(jnp.int32)
    ntile = (counts + tm - 1) // tm
    tile_end = jnp.cumsum(ntile)
    tile_start = tile_end - ntile
    n_used = tile_end[-1]
    ti = jnp.minimum(jnp.arange(n_tiles, dtype=jnp.int32), n_used - 1)
    te = jnp.minimum(jnp.sum(tile_end[None, :] <= ti[:, None], axis=1), N_EXPERTS - 1).astype(jnp.int32)
    onehot_te = te[:, None] == jnp.arange(N_EXPERTS, dtype=jnp.int32)[None, :]
    start_te = jnp.sum(jnp.where(onehot_te, tile_start[None, :], 0), axis=1)
    count_te = jnp.sum(jnp.where(onehot_te, counts[None, :], 0), axis=1)
    first = (jnp.arange(n_tiles, dtype=jnp.int32) == start_te).astype(jnp.int32)
    r0 = ((ti - start_te) * tm).astype(jnp.int32)
    nvalid = jnp.clip(count_te - r0, 0, tm).astype(jnp.int32)
    ar = jnp.arange(N_EXPERTS, dtype=jnp.int32)
    later_used = jnp.logical_and(ar[None, :] > ar[:, None], (counts > 0)[None, :])
    next_used = jnp.min(jnp.where(later_used, ar[None, :], N_EXPERTS), axis=1)
    nxt_e = jnp.sum(jnp.where(onehot_te, next_used[None, :], 0), axis=1).astype(jnp.int32)
    whole = lambda shape: pl.BlockSpec(shape, lambda g, *_: (0,) * len(shape))
    out_blk = pl.BlockSpec((SLOT_GROUP, 1, tm), lambda g, *_: (g, 0, 0))
    src, dst = pl.pallas_call(
        functools.partial(_slots_kernel, n=n),
        grid_spec=pltpu.PrefetchScalarGridSpec(
            num_scalar_prefetch=2,
            grid=(n_tiles // SLOT_GROUP,),
            in_specs=[whole((N_EXPERTS, nb, TOKEN_BLOCK)), whole((N_EXPERTS, nb, TOKEN_BLOCK)),
                      whole((N_EXPERTS, 1, nb)), whole((N_EXPERTS, nb, 1)), whole((N_EXPERTS, nb, 1))],
            out_specs=[out_blk, out_blk],
        ),
        out_shape=[jax.ShapeDtypeStruct((n_tiles, 1, tm), jnp.int32)] * 2,
        compiler_params=_params(("arbitrary",)),
        name="moe_slots",
    )(te, r0, lc3, ki.reshape(N_EXPERTS, nb, TOKEN_BLOCK), pin.reshape(N_EXPERTS, 1, nb),
      pex_hi.reshape(N_EXPERTS, nb, 1), (pex - pex_hi * PREFIX_RADIX).reshape(N_EXPERTS, nb, 1))
    return te, first, nvalid, nxt_e, n_used.reshape(1).astype(jnp.int32), src, dst


def _moe_kernel(te_ref, first_ref, nvalid_ref, nxte_ref, nused_ref, src_cur, src_nxt, dst_cur, dst_prv, h_hbm,
                wg_hbm, wu_hbm, wd_hbm, y_hbm, xbuf0, xbuf1, ybuf0, ybuf1, wgf, wuf, wdf, wgb, wub, wdb,
                gsem, ssem, wsem, *, layer):
    i = pl.program_id(0)
    tm = xbuf0.shape[0]
    n_used = nused_ref[0]
    nv = nvalid_ref[i]
    xbufs, ybufs = (xbuf0, xbuf1), (ybuf0, ybuf1)

    def gather(src_ref, p):
        for r in range(tm):
            pltpu.make_async_copy(h_hbm.at[pl.ds(src_ref[0, 0, r], 1)], xbufs[p].at[pl.ds(r, 1)],
                                  gsem.at[p]).start()

    def gather_wait(p):
        pltpu.make_async_copy(h_hbm.at[pl.ds(0, tm)], xbufs[p], gsem.at[p]).wait()

    def scatter_row(dst_ref, p, r):
        pltpu.make_async_copy(ybufs[p].at[pl.ds(r, 1)], y_hbm.at[pl.ds(dst_ref[0, 0, r], 1)],
                              ssem.at[p]).start()

    def scatter_wait(p, rows):
        @pl.when(rows == tm)
        def _():
            pltpu.make_async_copy(ybufs[p], y_hbm.at[pl.ds(0, tm)], ssem.at[p]).wait()

        @pl.when(rows < tm)
        def _():
            def wait_rows(count):
                def body(r, carry):
                    pltpu.make_async_copy(ybufs[p].at[pl.ds(0, count)], y_hbm.at[pl.ds(0, count)],
                                          ssem.at[p]).wait()
                    return carry
                return body

            lax.fori_loop(0, rows // ROW_GROUP, wait_rows(ROW_GROUP), 0)
            lax.fori_loop(0, rows % ROW_GROUP, wait_rows(1), 0)

    def weight_copies(e):
        return [pltpu.make_async_copy(hbm.at[layer, e], buf, wsem.at[k])
                for k, (hbm, buf) in enumerate(((wg_hbm, wgf), (wu_hbm, wuf), (wd_hbm, wdf)))]

    def step(p):
        @pl.when(i >= 2)
        def _():
            scatter_wait(p, nvalid_ref[i - 2])

        gather_wait(p)

        @pl.when(first_ref[i] == 1)
        def _():
            @pl.when(i == 0)
            def _():
                for cp in weight_copies(te_ref[0]):
                    cp.start(priority=1)

            for cp in weight_copies(te_ref[i]):
                cp.wait()
            wgb[...] = wgf[...].astype(BF16)
            wub[...] = wuf[...].astype(BF16)
            wdb[...] = wdf[...].astype(BF16)
            nxt = nxte_ref[i]

            @pl.when(nxt < N_EXPERTS)
            def _():
                for cp in weight_copies(nxt):
                    cp.start(priority=1)

        def main(scatter_prev):
            gather(src_nxt, 1 - p)
            if scatter_prev:
                for r in range(tm):
                    scatter_row(dst_prv, 1 - p, r)
            x = xbufs[p][...].astype(BF16)
            act = _silu(_bdot(x, wgb[...])) * _bdot(x, wub[...])
            ybufs[p][...] = _bdot(act.astype(BF16), wdb[...])

        prev_full = jnp.logical_and(i >= 1, nvalid_ref[jnp.maximum(i - 1, 0)] == tm)

        @pl.when(prev_full)
        def _():
            main(True)

        @pl.when(jnp.logical_not(prev_full))
        def _():
            main(False)

        last = i == n_used - 1

        @pl.when(jnp.logical_or(nv < tm, last))
        def _():
            def group_body(g, carry):
                for u in range(ROW_GROUP):
                    scatter_row(dst_cur, p, g * ROW_GROUP + u)
                return carry

            def row_body(r, carry):
                scatter_row(dst_cur, p, r)
                return carry

            groups = nv // ROW_GROUP
            lax.fori_loop(0, groups, group_body, 0)
            lax.fori_loop(groups * ROW_GROUP, nv, row_body, 0)

        @pl.when(last)
        def _():
            gather_wait(1 - p)
            scatter_wait(p, nv)

            @pl.when(i >= 1)
            def _():
                scatter_wait(1 - p, nvalid_ref[i - 1])

    @pl.when(i < n_used)
    def _():
        @pl.when(i == 0)
        def _():
            ybuf1[...] = jnp.zeros_like(ybuf1)
            tail0 = y_hbm.shape[0] - Y_TAIL_TILES * tm
            fills = [pltpu.make_async_copy(ybuf1, y_hbm.at[pl.ds(tail0 + j * tm, tm)], ssem.at[1])
                     for j in range(Y_TAIL_TILES)]
            for cp in fills:
                cp.start()
            for cp in fills:
                cp.wait()
            gather(src_cur, 0)

        for p in range(2):
            @pl.when(i % 2 == p)
            def _():
                step(p)


def _moe(h2, plan, w_gate, w_up, w_down, layer):
    n, d = h2.shape
    te, first, nvalid, nxt_e, n_used, src, dst = plan
    n_tiles, _, tm = src.shape
    f = w_gate.shape[3]
    idx_blk = lambda fn: pl.BlockSpec((1, 1, tm), fn, memory_space=pltpu.SMEM)
    hbm = pl.BlockSpec(memory_space=pl.ANY)
    return pl.pallas_call(
        functools.partial(_moe_kernel, layer=layer),
        grid_spec=pltpu.PrefetchScalarGridSpec(
            num_scalar_prefetch=5,
            grid=(n_tiles,),
            in_specs=[idx_blk(lambda i, *_: (i, 0, 0)),
                      idx_blk(lambda i, *_: (jnp.minimum(i + 1, n_tiles - 1), 0, 0)),
                      idx_blk(lambda i, *_: (i, 0, 0)),
                      idx_blk(lambda i, *_: (jnp.maximum(i - 1, 0), 0, 0)),
                      hbm, hbm, hbm, hbm],
            out_specs=hbm,
            scratch_shapes=[pltpu.VMEM((tm, d), F32)] * 4
                           + [pltpu.VMEM((d, f), F32), pltpu.VMEM((d, f), F32), pltpu.VMEM((f, d), F32),
                              pltpu.VMEM((d, f), BF16), pltpu.VMEM((d, f), BF16), pltpu.VMEM((f, d), BF16),
                              pltpu.SemaphoreType.DMA((2,)), pltpu.SemaphoreType.DMA((2,)),
                              pltpu.SemaphoreType.DMA((3,))],
        ),
        out_shape=jax.ShapeDtypeStruct((TOP_K * n + Y_TAIL_TILES * tm, d), F32),
        compiler_params=_params(("arbitrary",)),
        name="moe_experts",
    )(te, first, nvalid, nxt_e, n_used, src, src, dst, dst, h2, w_gate, w_up, w_down)


def _combine_kernel(*refs, alpha):
    y_refs = refs[:TOP_K]
    wt_ref, h_ref, x_ref, gf_ref, sg_ref, su_ref, sd_ref, g2_ref, b2_ref, o_ref = refs[TOP_K:]
    wt = wt_ref[...]
    routed = wt[:, 0:1] * y_refs[0][...]
    for k in range(1, TOP_K):
        routed = routed + wt[:, k:k + 1] * y_refs[k][...]
    hb = h_ref[...].astype(BF16)
    act = _silu(_bdot(hb, sg_ref[...])) * _bdot(hb, su_ref[...])
    y = routed + _bdot(act.astype(BF16), sd_ref[...])
    o_ref[...] = _ln(alpha * x_ref[...] + (1.0 + gf_ref[0]) * y) * g2_ref[...] + b2_ref[...]


def _combine(y_tok, wt, h2, x1, gf, sg, su, sd, g2, b2, alpha, seq):
    n, d = h2.shape
    tm = min(COMBINE_TILE, seq)
    nblk = n // tm
    row = pl.BlockSpec((tm, d), lambda i: (i, 0))
    const = lambda shape: pl.BlockSpec(shape, lambda i: (0, 0))
    y_specs = [pl.BlockSpec((tm, d), (lambda i, k=k: (k * nblk + i, 0))) for k in range(TOP_K)]
    return pl.pallas_call(
        functools.partial(_combine_kernel, alpha=alpha),
        grid=(nblk,),
        in_specs=y_specs + [pl.BlockSpec((tm, LANES), lambda i: (i, 0)), row, row,
                            pl.BlockSpec((1, 1, d), lambda i: (i // (seq // tm), 0, 0)),
                            const(sg.shape), const(su.shape), const(sd.shape), const((1, d)), const((1, d))],
        out_specs=row,
        out_shape=jax.ShapeDtypeStruct((n, d), F32),
        compiler_params=_params(("arbitrary",)),
        name="combine",
    )(*([y_tok] * TOP_K), wt, h2, x1, gf, sg, su, sd, g2.reshape(1, d), b2.reshape(1, d))


def _rope_tables(s, half, width, offset, period=None):
    inv = ROPE_THETA ** (-jnp.arange(half, dtype=F32) / half)
    ang = jnp.arange(s, dtype=jnp.int32).astype(F32)[:, None] * inv[None, :]
    cos, sin = jnp.cos(ang), jnp.sin(ang)
    period = period or width
    tail = period - offset - 2 * half
    one = lambda n: jnp.ones((s, n), F32)
    zero = lambda n: jnp.zeros((s, n), F32)
    c = jnp.concatenate([one(offset), cos, cos, one(tail)], axis=1)
    s1 = jnp.concatenate([zero(offset), -sin, zero(half + tail)], axis=1)
    s2 = jnp.concatenate([zero(offset + half), sin, zero(tail)], axis=1)
    reps = width // period
    return tuple(jnp.tile(t, (1, reps)) for t in (c, s1, s2))


def kernel(x, c, w_ada, b_ada, w_in, mla_q_norm, mla_w_uq, mla_kv_norm, mla_w_ukv, na_rpb, diff_lq1, diff_lk1,
           diff_lq2, diff_lk2, diff_subln, w_o, ln1_g, ln1_b, router_w, router_bias, exp_w_gate, exp_w_up,
           exp_w_down, sh_w_gate, sh_w_up, sh_w_down, ln2_g, ln2_b):
    b, s, d = x.shape
    depth = w_ada.shape[0]
    n = b * s
    alpha = (2 * depth) ** 0.25
    ada = _ada(c, w_ada, b_ada)
    tab_diff = _rope_tables(s, DIFF_ROT // 2, LANES, 0, period=DIFF_QK)
    tab_q = _rope_tables(s, MLA_ROPE // 2, MLA_HEAD_PAD, MLA_NOPE)
    tab_k = _rope_tables(s, MLA_ROPE // 2, LANES, 0)
    a_pad = LANES - MLA_ROPE

    for i in range(depth):
        lam_init = 0.8 - 0.6 * math.exp(-0.3 * i)
        sh_a, sc_a, g_a, sh_f, sc_f, g_f = [a[:, None, :] for a in jnp.split(ada[i], 6, -1)]
        wi = w_in[i]
        w_a = jnp.pad(wi[:, :IN_A], ((0, 0), (0, a_pad))).astype(BF16)
        w_b = wi[:, IN_A:IN_A + IN_B].astype(BF16)
        w_c = wi[:, IN_A + IN_B:].astype(BF16)
        wq = jnp.pad(mla_w_uq[i].reshape(MLA_Q_RANK, MLA_HEADS, MLA_NOPE + MLA_ROPE),
                     ((0, 0), (0, 0), (0, MLA_HEAD_PAD - MLA_NOPE - MLA_ROPE)))
        wq = wq.reshape(MLA_Q_RANK, MLA_HEADS * MLA_HEAD_PAD).astype(BF16)
        wkv3 = mla_w_ukv[i].reshape(MLA_KV_RANK, MLA_HEADS, MLA_NOPE + MLA_V)
        wkv = jnp.concatenate([wkv3[:, :, :MLA_NOPE].reshape(MLA_KV_RANK, -1),
                               wkv3[:, :, MLA_NOPE:].reshape(MLA_KV_RANK, -1)], axis=1).astype(BF16)
        rw = jnp.pad(router_w[i], ((0, 0), (0, LANES - N_EXPERTS)))

        ua, ub, uc, kt_c = _inproj(x, sc_a, sh_a, w_a, w_b, w_c, tab_diff)
        q_a, kt_a, v_a = _mla_proj(ua, mla_q_norm[i], mla_kv_norm[i], wq, wkv, tab_q, tab_k)
        o_a = _mla_attn(q_a, kt_a, v_a)
        o_b = _na_attn(ub, _na_bias(na_rpb[i], s // GRID_W))
        o_c = _diff_attn(uc, kt_c, diff_lq1[i], diff_lk1[i], diff_lq2[i], diff_lk2[i], diff_subln[i], lam_init)
        x1, h2, logits = _oproj(o_a, o_b, o_c, x, g_a, w_o[i].astype(BF16), ln1_g[i], ln1_b[i], sc_f, sh_f,
                                rw, alpha)

        h2 = h2.reshape(n, d)
        wt, lc, ki = _router(logits.reshape(n, LANES), router_bias[i])
        plan = _moe_plan(lc, ki, n)
        y_tok = _moe(h2, plan, exp_w_gate, exp_w_up, exp_w_down, i)
        x = _combine(y_tok, wt, h2, x1.reshape(n, d), g_f, sh_w_gate[i].astype(BF16), sh_w_up[i].astype(BF16),
                     sh_w_down[i].astype(BF16), ln2_g[i], ln2_b[i], alpha, s).reshape(b, s, d)
    return x
```
